```python
import math
import jax, jax.numpy as jnp
from jax import lax
import numpy as np


D_MODEL = 1024
BATCH = 8
SEQ = 8192
DEPTH = 2

CTX_LEN = 256
GRID_W = 64
D_MIX = D_MODEL
CONV_CH = D_MIX // 2
CONV_K = 31
QK_NOPE = 64
QK_ROPE = 32
V_DIM = 64
MLA_HEADS = (D_MIX - CONV_CH) // V_DIM
QK_DIM = QK_NOPE + QK_ROPE
Q_LORA = 384
KV_LORA = 256
ROPE_THETA = 10000.0
SHORT_K = 3
FFN_DIM = 2816
FFN_K = 3
Q_BLOCK = 128
EPS = 1e-6
N_MOD = 6
EVEN_IN = 2 * CONV_CH + Q_LORA + KV_LORA + QK_ROPE
ODD_IN = 3 * D_MIX
SM_SCALE = QK_DIM ** -0.5

kernel_name = "hybrid_conformer_mla_shortconv_dit"


def rms_norm(x, g):
    xf = x.astype(jnp.float32)
    y = xf * lax.rsqrt(jnp.mean(xf * xf, axis=-1, keepdims=True) + EPS)
    return (y * g.astype(jnp.float32)).astype(x.dtype)


def layer_norm(x, g, b):
    xf = x.astype(jnp.float32)
    mu = jnp.mean(xf, axis=-1, keepdims=True)
    xc = xf - mu
    y = xc * lax.rsqrt(jnp.mean(xc * xc, axis=-1, keepdims=True) + EPS)
    return (y * g.astype(jnp.float32) + b.astype(jnp.float32)).astype(x.dtype)


def modulate(h, shift, scale):
    return h * (1 + scale) + shift


def dwconv(x, w, b):
    y = lax.conv_general_dilated(
        x, w[:, None, :].astype(x.dtype), window_strides=(1,), padding='SAME',
        dimension_numbers=('NWC', 'WIO', 'NWC'), feature_group_count=x.shape[-1])
    return y + b.astype(x.dtype)


def axial_tables(n_tokens):
    rows = n_tokens // GRID_W
    row = jnp.broadcast_to(jnp.arange(rows)[:, None], (rows, GRID_W)).reshape(-1)
    col = jnp.broadcast_to(jnp.arange(GRID_W)[None, :], (rows, GRID_W)).reshape(-1)
    half = QK_ROPE // 2
    inv = ROPE_THETA ** (-jnp.arange(0, half, 2, dtype=jnp.float32) / half)

    def cos_sin(pos):
        ang = pos.astype(jnp.float32)[:, None] * inv[None, :]
        ang = jnp.concatenate([ang, ang], axis=-1)
        return jnp.cos(ang), jnp.sin(ang)

    return cos_sin(row), cos_sin(col)


def rotate_half(v):
    v1, v2 = jnp.split(v, 2, axis=-1)
    return jnp.concatenate([-v2, v1], axis=-1)


def apply_axial_rope(x, tables):
    (cr, sr), (cc, sc) = tables
    half = QK_ROPE // 2

    def rot(v, cs, sn):
        vf = v.astype(jnp.float32)
        out = vf * cs[None, :, None, :] + rotate_half(vf) * sn[None, :, None, :]
        return out.astype(v.dtype)

    return jnp.concatenate([rot(x[..., :half], cr, sr), rot(x[..., half:], cc, sc)], axis=-1)


def split_even(proj):
    o1 = 2 * CONV_CH
    o2 = o1 + Q_LORA
    o3 = o2 + KV_LORA
    return proj[..., :o1], proj[..., o1:o2], proj[..., o2:o3], proj[..., o3:]


def conformer_conv(glu_in, conv_w, conv_b, ln_g, ln_b):
    val, gate = jnp.split(glu_in, 2, axis=-1)
    u = dwconv(val * jax.nn.sigmoid(gate), conv_w, conv_b)
    return jax.nn.silu(layer_norm(u, ln_g, ln_b))


def mla_q(cq, qa_g, w_uq, q_g, tables):
    b, l = cq.shape[:2]
    q = (rms_norm(cq, qa_g) @ w_uq).reshape(b, l, MLA_HEADS, QK_DIM)
    q = rms_norm(q, q_g)
    if tables is not None:
        q = jnp.concatenate([q[..., :QK_NOPE], apply_axial_rope(q[..., QK_NOPE:], tables)], axis=-1)
    return q


def mla_kv(ckv, kr, kva_g, w_ukv, k_g, tables):
    b, l = ckv.shape[:2]
    kv = (rms_norm(ckv, kva_g) @ w_ukv).reshape(b, l, MLA_HEADS, QK_NOPE + V_DIM)
    k_nope, v = kv[..., :QK_NOPE], kv[..., QK_NOPE:]
    k_rope = jnp.broadcast_to(kr[:, :, None, :], (b, l, MLA_HEADS, QK_ROPE))
    k = rms_norm(jnp.concatenate([k_nope, k_rope], axis=-1), k_g)
    if tables is not None:
        k = jnp.concatenate([k[..., :QK_NOPE], apply_axial_rope(k[..., QK_NOPE:], tables)], axis=-1)
    return k, v


def attend_blocks(q, k, v):
    b, s, h, dq = q.shape
    nb = s // Q_BLOCK
    qb = q.reshape(b, nb, Q_BLOCK, h, dq).transpose(1, 0, 2, 3, 4)

    def one_block(qi):
        sc = jnp.einsum('bqhd,bkhd->bhqk', qi, k, preferred_element_type=jnp.float32) * SM_SCALE
        p = jax.nn.softmax(sc, axis=-1).astype(v.dtype)
        return jnp.einsum('bhqk,bkhd->bqhd', p, v)

    o = lax.map(one_block, qb)
    return o.transpose(1, 0, 2, 3, 4).reshape(b, s, h * v.shape[-1])


def short_conv_mixer(h, w_in, conv_w, conv_b, w_out):
    bg, cg, u = jnp.split(h @ w_in, 3, axis=-1)
    return (bg * dwconv(cg * u, conv_w, conv_b)) @ w_out


def conv_ffn(h, w_up, conv_w, conv_b, w_down):
    gate, val = jnp.split(h @ w_up, 2, axis=-1)
    return (jax.nn.silu(dwconv(gate, conv_w, conv_b)) * val) @ w_down


def setup_inputs(seed: int = 0) -> dict:
    key = jax.random.key(seed)
    ks = iter(jax.random.split(key, 40))
    f32 = jnp.float32
    D = D_MODEL
    ne = (DEPTH + 1) // 2
    no = DEPTH // 2

    def nrm(shape, scale):
        return scale * jax.random.normal(next(ks), shape, f32)

    def gain(shape):
        return 1.0 + 0.02 * jax.random.normal(next(ks), shape, f32)

    return {
        "x": nrm((BATCH, SEQ, D), 1.0),
        "c": nrm((BATCH, D), 1.0),
        "ctx": nrm((BATCH, CTX_LEN, D), 1.0),
        "c_ctx": nrm((D,), 1.0),
        "ada_w": nrm((DEPTH, D, N_MOD * D), 0.5 * D ** -0.5),
        "ada_b": nrm((DEPTH, N_MOD * D), 0.01),
        "norm_mix_g": gain((DEPTH, D)),
        "norm_ffn_g": gain((DEPTH, D)),
        "ffn_w_up": nrm((DEPTH, D, 2 * FFN_DIM), D ** -0.5),
        "ffn_conv_w": nrm((DEPTH, FFN_K, FFN_DIM), FFN_K ** -0.5),
        "ffn_conv_b": nrm((DEPTH, FFN_DIM), 0.01),
        "ffn_w_down": nrm((DEPTH, FFN_DIM, D), FFN_DIM ** -0.5),
        "ev_w_in": nrm((ne, D, EVEN_IN), D ** -0.5),
        "ev_conv_w": nrm((ne, CONV_K, CONV_CH), CONV_K ** -0.5),
        "ev_conv_b": nrm((ne, CONV_CH), 0.01),
        "ev_ln_g": gain((ne, CONV_CH)),
        "ev_ln_b": nrm((ne, CONV_CH), 0.01),
        "ev_qa_norm_g": gain((ne, Q_LORA)),
        "ev_w_uq": nrm((ne, Q_LORA, MLA_HEADS * QK_DIM), Q_LORA ** -0.5),
        "ev_kva_norm_g": gain((ne, KV_LORA)),
        "ev_w_ukv": nrm((ne, KV_LORA, MLA_HEADS * (QK_NOPE + V_DIM)), KV_LORA ** -0.5),
        "ev_q_norm_g": gain((ne, QK_DIM)),
        "ev_k_norm_g": gain((ne, QK_DIM)),
        "ev_w_out": nrm((ne, D_MIX, D), D_MIX ** -0.5),
        "od_w_in": nrm((no, D, ODD_IN), D ** -0.5),
        "od_conv_w": nrm((no, SHORT_K, D_MIX), SHORT_K ** -0.5),
        "od_conv_b": nrm((no, D_MIX), 0.01),
        "od_w_out": nrm((no, D_MIX, D), D_MIX ** -0.5),
    }


def reference(x, c, ctx, c_ctx, ada_w, ada_b, norm_mix_g, norm_ffn_g, ffn_w_up, ffn_conv_w, ffn_conv_b,
              ffn_w_down, ev_w_in, ev_conv_w, ev_conv_b, ev_ln_g, ev_ln_b, ev_qa_norm_g, ev_w_uq,
              ev_kva_norm_g, ev_w_ukv, ev_q_norm_g, ev_k_norm_g, ev_w_out, od_w_in, od_conv_w, od_conv_b,
              od_w_out):
    tables = axial_tables(x.shape[1])
    silu_c = jax.nn.silu(c)
    silu_cc = jax.nn.silu(c_ctx)
    xc = ctx

    for layer in range(DEPTH):
        i = layer // 2
        is_even = layer % 2 == 0
        update_ctx = any(j % 2 == 0 for j in range(layer + 1, DEPTH))

        mod_lat = (silu_c @ ada_w[layer] + ada_b[layer])[:, None, :]
        sh_m, sc_m, g_m, sh_f, sc_f, g_f = jnp.split(mod_lat, N_MOD, axis=-1)
        if is_even or update_ctx:
            mod_ctx = (silu_cc @ ada_w[layer] + ada_b[layer])[None, None, :]
            csh_m, csc_m, cg_m, csh_f, csc_f, cg_f = jnp.split(mod_ctx, N_MOD, axis=-1)
            hc = modulate(rms_norm(xc, norm_mix_g[layer]), csh_m, csc_m)

        h = modulate(rms_norm(x, norm_mix_g[layer]), sh_m, sc_m)

        if is_even:
            glu_c, cq_c, ckv_c, kr_c = split_even(hc @ ev_w_in[i])
            k_c, v_c = mla_kv(ckv_c, kr_c, ev_kva_norm_g[i], ev_w_ukv[i], ev_k_norm_g[i], None)

            glu, cq, ckv, kr = split_even(h @ ev_w_in[i])
            a = conformer_conv(glu, ev_conv_w[i], ev_conv_b[i], ev_ln_g[i], ev_ln_b[i])
            q = mla_q(cq, ev_qa_norm_g[i], ev_w_uq[i], ev_q_norm_g[i], tables)
            k, v = mla_kv(ckv, kr, ev_kva_norm_g[i], ev_w_ukv[i], ev_k_norm_g[i], tables)
            k_all = jnp.concatenate([k_c, k], axis=1)
            v_all = jnp.concatenate([v_c, v], axis=1)
            att = attend_blocks(q, k_all, v_all)
            x = x + g_m * (jnp.concatenate([a, att], axis=-1) @ ev_w_out[i])

            if update_ctx:
                a_c = conformer_conv(glu_c, ev_conv_w[i], ev_conv_b[i], ev_ln_g[i], ev_ln_b[i])
                q_c = mla_q(cq_c, ev_qa_norm_g[i], ev_w_uq[i], ev_q_norm_g[i], None)
                att_c = attend_blocks(q_c, k_c, v_c)
                xc = xc + cg_m * (jnp.concatenate([a_c, att_c], axis=-1) @ ev_w_out[i])
        else:
            x = x + g_m * short_conv_mixer(h, od_w_in[i], od_conv_w[i], od_conv_b[i], od_w_out[i])
            if update_ctx:
                xc = xc + cg_m * short_conv_mixer(hc, od_w_in[i], od_conv_w[i], od_conv_b[i], od_w_out[i])

        hf = modulate(rms_norm(x, norm_ffn_g[layer]), sh_f, sc_f)
        x = x + g_f * conv_ffn(hf, ffn_w_up[layer], ffn_conv_w[layer], ffn_conv_b[layer], ffn_w_down[layer])
        if update_ctx:
            hfc = modulate(rms_norm(xc, norm_ffn_g[layer]), csh_f, csc_f)
            xc = xc + cg_f * conv_ffn(hfc, ffn_w_up[layer], ffn_conv_w[layer], ffn_conv_b[layer], ffn_w_down[layer])

    return x
```

```python
import functools

import jax
import jax.numpy as jnp
from jax import lax
from jax.experimental import pallas as pl
from jax.experimental.pallas import tpu as pltpu

F32 = jnp.float32
BF16 = jnp.bfloat16

GRID_W = 64
QK_NOPE = 64
QK_ROPE = 32
V_DIM = 64
QK_DIM = QK_NOPE + QK_ROPE
ROPE_THETA = 10000.0
EPS = 1e-6
SM_SCALE = QK_DIM ** -0.5
N_MOD = 6

LANES = 128
MXU_WIDTH = 256
VMEM_LIMIT = 60000 * 1024

HALO = 8
CONF_HALO = 16
NEG_BIG = -1e30


def _tile(n, pref):
    t = min(n, pref)
    while n % t:
        t //= 2
    return t


def _rms(x, g, n):
    ms = jnp.sum(x * x, axis=-1, keepdims=True) * (1.0 / n)
    return x * lax.rsqrt(ms + EPS) * g


def _silu(x):
    return x * jax.nn.sigmoid(x)


def _params(sem):
    return pltpu.CompilerParams(dimension_semantics=sem, vmem_limit_bytes=VMEM_LIMIT)


def _mod_kernel(c_ref, w_ref, b_ref, o_ref):
    s = _silu(c_ref[...])
    o_ref[0] = jnp.dot(s, w_ref[0], preferred_element_type=F32,
                       precision=lax.Precision.HIGHEST) + b_ref[0]


def _mod_call(cvec, ada_w, ada_b):
    depth, d, nd = ada_w.shape
    rows = cvec.shape[0]
    tn = _tile(nd, 1536)
    return pl.pallas_call(
        _mod_kernel,
        grid=(depth, nd // tn),
        in_specs=[
            pl.BlockSpec((rows, d), lambda l, j: (0, 0)),
            pl.BlockSpec((1, d, tn), lambda l, j: (l, 0, j)),
            pl.BlockSpec((1, 1, tn), lambda l, j: (l, 0, j)),
        ],
        out_specs=pl.BlockSpec((1, rows, tn), lambda l, j: (l, 0, j)),
        out_shape=jax.ShapeDtypeStruct((depth, rows, nd), F32),
        compiler_params=_params(("arbitrary", "arbitrary")),
        name="mod",
    )(cvec, ada_w, ada_b.reshape(depth, 1, nd))


def _rope(x, cos, sina, sinb):
    return x * cos + pltpu.roll(x, LANES - 8, 1) * sina + pltpu.roll(x, 8, 1) * sinb


def _even_in_kernel(x_ref, sh_ref, sc_ref, gmix_ref, w1_ref, qag_ref, wq_ref, kvag_ref, wk_ref, wv_ref,
                    gq_ref, gk_ref, cos_ref, sina_ref, sinb_ref, *out_refs, heads, conv_ch, q_lora,
                    kv_lora, with_q):
    if with_q:
        u_ref, q_ref, kt_ref, v_ref = out_refs
    else:
        kt_ref, v_ref = out_refs
    ts, d = x_ref.shape[1], x_ref.shape[2]
    x = x_ref[0]
    h = _rms(x, gmix_ref[...], d) * (1.0 + sc_ref[0]) + sh_ref[0]
    proj = jnp.dot(h.astype(BF16), w1_ref[...], preferred_element_type=F32)
    o_cq = 2 * conv_ch
    o_ckv = o_cq + q_lora
    o_kr = o_ckv + kv_lora
    cos, sina, sinb = cos_ref[...], sina_ref[...], sinb_ref[...]
    lane = lax.broadcasted_iota(jnp.int32, (1, LANES), 1)
    nblk = ts // LANES

    if with_q:
        u_ref[0] = proj[:, :conv_ch] * jax.nn.sigmoid(proj[:, conv_ch:o_cq])
        cqn = _rms(proj[:, o_cq:o_ckv], qag_ref[...], q_lora)
        qf = jnp.dot(cqn.astype(BF16), wq_ref[...], preferred_element_type=F32)
        gq = gq_ref[...]
        for hd in range(heads):
            qh = qf[:, hd * LANES:(hd + 1) * LANES]
            r = lax.rsqrt(jnp.sum(qh * qh, axis=-1, keepdims=True) * (1.0 / QK_DIM) + EPS)
            qr = _rope(qh * r * gq, cos, sina, sinb) * SM_SCALE
            q_ref[0, hd // 2, :, (hd % 2) * LANES:(hd % 2 + 1) * LANES] = qr.astype(BF16)

    ckvn = _rms(proj[:, o_ckv:o_kr], kvag_ref[...], kv_lora).astype(BF16)
    kf = jnp.dot(ckvn, wk_ref[...], preferred_element_type=F32)
    vf = jnp.dot(ckvn, wv_ref[...], preferred_element_type=F32)
    krt = proj[:, o_kr:o_kr + LANES]
    gk = gk_ref[...]
    zero_blk = jnp.zeros((LANES, LANES), BF16)
    for hd in range(heads):
        kh = kf[:, hd * LANES:(hd + 1) * LANES] + krt
        r = lax.rsqrt(jnp.sum(kh * kh, axis=-1, keepdims=True) * (1.0 / QK_DIM) + EPS)
        kt = _rope(kh * r * gk, cos, sina, sinb).T.astype(BF16)
        p, half = hd // 2, hd % 2
        for j in range(nblk):
            c0 = j * MXU_WIDTH + half * LANES
            kt_ref[0, p, half * LANES:(half + 1) * LANES, c0:c0 + LANES] = kt[:, j * LANES:(j + 1) * LANES]
            kt_ref[0, p, (1 - half) * LANES:(2 - half) * LANES, c0:c0 + LANES] = zero_blk

    for p in range(heads // 2):
        vp = vf[:, p * LANES:(p + 1) * LANES]
        for half in range(2):
            keep = (lane < V_DIM) if half == 0 else (lane >= V_DIM)
            vh = jnp.where(keep, vp, 0.0).astype(BF16)
            ones = jnp.broadcast_to(jnp.where(lane == half, 1.0, 0.0), (LANES, LANES)).astype(BF16)
            for j in range(nblk):
                r0 = j * MXU_WIDTH + half * LANES
                v_ref[0, p, r0:r0 + LANES, 0:LANES] = vh[j * LANES:(j + 1) * LANES]
                v_ref[0, p, r0:r0 + LANES, LANES:MXU_WIDTH] = ones


def _even_in_call(x, sh, sc, gmix, w1, qag, wq, kvag, wk, wv, gq, gk, cos, sina, sinb, *, heads, conv_ch,
                  q_lora, kv_lora, with_q, name):
    b, s, d = x.shape
    ts = _tile(s, 512)
    hp = heads // 2
    kern = functools.partial(_even_in_kernel, heads=heads, conv_ch=conv_ch, q_lora=q_lora, kv_lora=kv_lora,
                             with_q=with_q)
    full = lambda a: pl.BlockSpec(a.shape, lambda bi, i: (0,) * a.ndim)
    vec = pl.BlockSpec((1, 1, d), lambda bi, i: (bi, 0, 0))
    tab = pl.BlockSpec((ts, LANES), lambda bi, i: (i, 0))
    in_specs = [pl.BlockSpec((1, ts, d), lambda bi, i: (bi, i, 0)), vec, vec, full(gmix), full(w1), full(qag),
                full(wq), full(kvag), full(wk), full(wv), full(gq), full(gk), tab, tab, tab]
    out_specs = [pl.BlockSpec((1, hp, MXU_WIDTH, 2 * ts), lambda bi, i: (bi, 0, 0, i)),
                 pl.BlockSpec((1, hp, 2 * ts, MXU_WIDTH), lambda bi, i: (bi, 0, i, 0))]
    out_shape = [jax.ShapeDtypeStruct((b, hp, MXU_WIDTH, 2 * s), BF16),
                 jax.ShapeDtypeStruct((b, hp, 2 * s, MXU_WIDTH), BF16)]
    if with_q:
        out_specs = [pl.BlockSpec((1, ts, conv_ch), lambda bi, i: (bi, i, 0)),
                     pl.BlockSpec((1, hp, ts, MXU_WIDTH), lambda bi, i: (bi, 0, i, 0))] + out_specs
        out_shape = [jax.ShapeDtypeStruct((b, s, conv_ch), F32),
                     jax.ShapeDtypeStruct((b, hp, s, MXU_WIDTH), BF16)] + out_shape
    return pl.pallas_call(
        kern, grid=(b, s // ts), in_specs=in_specs, out_specs=out_specs, out_shape=out_shape,
        compiler_params=_params(("parallel", "arbitrary")), name=name,
    )(x, sh, sc, gmix, w1, qag, wq, kvag, wk, wv, gq, gk, cos, sina, sinb)


def _conf_conv_kernel(um_ref, up_ref, un_ref, w_ref, b_ref, g_ref, beta_ref, o_ref, buf_ref, *, taps, rc):
    i, n = pl.program_id(1), pl.num_programs(1)
    ts, ch = um_ref.shape[1], um_ref.shape[2]
    buf_ref[0:CONF_HALO] = jnp.where(i > 0, up_ref[0], 0.0)
    buf_ref[CONF_HALO:CONF_HALO + ts] = um_ref[0]
    buf_ref[CONF_HALO + ts:] = jnp.where(i < n - 1, un_ref[0], 0.0)
    pad = (taps - 1) // 2
    for c in range(ts // rc):
        acc = jnp.zeros((rc, ch), F32)
        for k in range(taps):
            acc = acc + w_ref[k:k + 1, :] * buf_ref[pl.ds(c * rc + CONF_HALO - pad + k, rc), :]
        u = acc + b_ref[...]
        mu = jnp.mean(u, axis=-1, keepdims=True)
        xc = u - mu
        y = xc * lax.rsqrt(jnp.mean(xc * xc, axis=-1, keepdims=True) + EPS) * g_ref[...] + beta_ref[...]
        o_ref[0, c * rc:(c + 1) * rc] = _silu(y).astype(o_ref.dtype)


def _conf_conv_call(u, w, bias, g, beta):
    b, s, ch = u.shape
    taps = w.shape[0]
    ts = _tile(s, 512)
    hb = ts // CONF_HALO
    nh = s // CONF_HALO
    kern = functools.partial(_conf_conv_kernel, taps=taps, rc=_tile(ts, 64))
    full = lambda a: pl.BlockSpec(a.shape, lambda bi, i: (0,) * a.ndim)
    return pl.pallas_call(
        kern, grid=(b, s // ts),
        in_specs=[pl.BlockSpec((1, ts, ch), lambda bi, i: (bi, i, 0)),
                  pl.BlockSpec((1, CONF_HALO, ch), lambda bi, i: (bi, jnp.maximum(i * hb - 1, 0), 0)),
                  pl.BlockSpec((1, CONF_HALO, ch), lambda bi, i: (bi, jnp.minimum((i + 1) * hb, nh - 1), 0)),
                  full(w), full(bias), full(g), full(beta)],
        out_specs=pl.BlockSpec((1, ts, ch), lambda bi, i: (bi, i, 0)),
        out_shape=jax.ShapeDtypeStruct((b, s, ch), BF16),
        scratch_shapes=[pltpu.VMEM((ts + 2 * CONF_HALO, ch), F32)],
        compiler_params=_params(("parallel", "arbitrary")), name="conf_conv",
    )(u, u, u, w, bias, g, beta)


def _attn_pair_step(p, q_ref, kt_ref, v_ref, acc_ref, m_ref):
    q = q_ref[0, p]
    s = jnp.dot(q, kt_ref[0, p], preferred_element_type=F32)
    nt = s.shape[1] // LANES
    tile = lambda t: s[:, t * LANES:(t + 1) * LANES]
    mx = [tile(0), tile(1)]
    for t in range(2, nt):
        mx[t % 2] = jnp.maximum(mx[t % 2], tile(t))
    m_new, alpha = [], []
    for half in range(2):
        m_old = m_ref[2 * p + half]
        mn = jnp.maximum(m_old, jnp.max(mx[half], axis=-1, keepdims=True))
        m_ref[2 * p + half] = mn
        m_new.append(mn)
        alpha.append(jnp.exp(m_old - mn))
    pm = jnp.concatenate([jnp.exp(tile(t) - m_new[t % 2]).astype(BF16) for t in range(nt)], axis=1)
    pv = jnp.dot(pm, v_ref[0, p], preferred_element_type=F32)
    lane = lax.broadcasted_iota(jnp.int32, (1, LANES), 1)
    a_lo = jnp.where(lane < V_DIM, alpha[0], alpha[1])
    a_hi = jnp.where(lane == 0, alpha[0], alpha[1])
    acc_ref[p, :, 0:LANES] = acc_ref[p, :, 0:LANES] * a_lo + pv[:, 0:LANES]
    acc_ref[p, :, LANES:MXU_WIDTH] = acc_ref[p, :, LANES:MXU_WIDTH] * a_hi + pv[:, LANES:MXU_WIDTH]


def _attn_kernel(q_ref, ktc_ref, vc_ref, kt_ref, v_ref, o_ref, acc_ref, m_ref):
    j, nj = pl.program_id(2), pl.num_programs(2)
    hp = q_ref.shape[1]

    @pl.when(j == 0)
    def _():
        acc_ref[...] = jnp.zeros(acc_ref.shape, F32)
        m_ref[...] = jnp.full(m_ref.shape, NEG_BIG, F32)

        def ctx_body(p, carry):
            _attn_pair_step(p, q_ref, ktc_ref, vc_ref, acc_ref, m_ref)
            return carry
        lax.fori_loop(0, hp, ctx_body, 0)

    def body(p, carry):
        _attn_pair_step(p, q_ref, kt_ref, v_ref, acc_ref, m_ref)
        return carry
    lax.fori_loop(0, hp, body, 0)

    @pl.when(j == nj - 1)
    def _():
        lane = lax.broadcasted_iota(jnp.int32, (1, LANES), 1)
        for p in range(hp):
            l0 = acc_ref[p, :, LANES:LANES + 1]
            l1 = acc_ref[p, :, LANES + 1:LANES + 2]
            inv = jnp.where(lane < V_DIM, 1.0 / l0, 1.0 / l1)
            o_ref[0, :, p * LANES:(p + 1) * LANES] = (acc_ref[p, :, 0:LANES] * inv).astype(o_ref.dtype)


def _attn_call(q, ktc, vc, kt, v):
    b, hp, s, _ = q.shape
    nctx2 = ktc.shape[3]
    tq = _tile(s, 1024)
    kbs = _tile(s, 512)
    return pl.pallas_call(
        _attn_kernel, grid=(b, s // tq, s // kbs),
        in_specs=[pl.BlockSpec((1, hp, tq, MXU_WIDTH), lambda bi, i, j: (bi, 0, i, 0)),
                  pl.BlockSpec((1, hp, MXU_WIDTH, nctx2), lambda bi, i, j: (bi, 0, 0, 0)),
                  pl.BlockSpec((1, hp, nctx2, MXU_WIDTH), lambda bi, i, j: (bi, 0, 0, 0)),
                  pl.BlockSpec((1, hp, MXU_WIDTH, 2 * kbs), lambda bi, i, j: (bi, 0, 0, j)),
                  pl.BlockSpec((1, hp, 2 * kbs, MXU_WIDTH), lambda bi, i, j: (bi, 0, j, 0))],
        out_specs=pl.BlockSpec((1, tq, hp * LANES), lambda bi, i, j: (bi, i, 0)),
        out_shape=jax.ShapeDtypeStruct((b, s, hp * LANES), BF16),
        scratch_shapes=[pltpu.VMEM((hp, tq, MXU_WIDTH), F32), pltpu.VMEM((2 * hp, tq, LANES), F32)],
        compiler_params=_params(("parallel", "parallel", "arbitrary")), name="attn",
    )(q, ktc, vc, kt, v)


def _out_proj_kernel(x_ref, a_ref, t_ref, g_ref, wa_ref, wt_ref, o_ref):
    y = jnp.dot(a_ref[0], wa_ref[...], preferred_element_type=F32)
    y = y + jnp.dot(t_ref[0], wt_ref[...], preferred_element_type=F32)
    o_ref[0] = x_ref[0] + g_ref[0] * y


def _out_proj_call(x, a, att, g, wa, wt):
    b, s, d = x.shape
    ts = _tile(s, 1024)
    row = lambda c: pl.BlockSpec((1, ts, c), lambda bi, i: (bi, i, 0))
    full = lambda arr: pl.BlockSpec(arr.shape, lambda bi, i: (0,) * arr.ndim)
    return pl.pallas_call(
        _out_proj_kernel, grid=(b, s // ts),
        in_specs=[row(d), row(a.shape[2]), row(att.shape[2]),
                  pl.BlockSpec((1, 1, d), lambda bi, i: (bi, 0, 0)), full(wa), full(wt)],
        out_specs=row(d), out_shape=jax.ShapeDtypeStruct((b, s, d), F32),
        compiler_params=_params(("parallel", "arbitrary")), name="out_proj",
    )(x, a, att, g, wa, wt)


def _halo_hidden(xm_ref, xp_ref, xn_ref, sh_ref, sc_ref, ng_ref):
    i, n = pl.program_id(1), pl.num_programs(1)
    ts, d = xm_ref.shape[1], xm_ref.shape[2]
    xe = jnp.concatenate([xp_ref[0], xm_ref[0], xn_ref[0]], axis=0)
    h = _rms(xe, ng_ref[...], d) * (1.0 + sc_ref[0]) + sh_ref[0]
    row = lax.broadcasted_iota(jnp.int32, (ts + 2 * HALO, 1), 0)
    lo = jnp.where(i > 0, 0, HALO)
    hi = jnp.where(i < n - 1, ts + 2 * HALO, ts + HALO)
    valid = jnp.logical_and(row >= lo, row < hi)
    return jnp.where(valid, h, 0.0).astype(BF16)


def _conv3(z, w, ts):
    rows = z.shape[0]
    y = w[0:1] * pltpu.roll(z, 1, 0) + w[1:2] * z + w[2:3] * pltpu.roll(z, rows - 1, 0)
    return y[HALO:HALO + ts]


def _halo_specs(ts, s, d):
    hb, nh = ts // HALO, s // HALO
    return [pl.BlockSpec((1, ts, d), lambda bi, i: (bi, i, 0)),
            pl.BlockSpec((1, HALO, d), lambda bi, i: (bi, jnp.maximum(i * hb - 1, 0), 0)),
            pl.BlockSpec((1, HALO, d), lambda bi, i: (bi, jnp.minimum((i + 1) * hb, nh - 1), 0))]


def _ffn_kernel(xm_ref, xp_ref, xn_ref, sh_ref, sc_ref, g_ref, ng_ref, wup_ref, cw_ref, cb_ref, wdn_ref,
                o_ref, acc_ref, h_ref):
    ts = xm_ref.shape[1]
    fc = wdn_ref.shape[1]
    h_ref[...] = _halo_hidden(xm_ref, xp_ref, xn_ref, sh_ref, sc_ref, ng_ref)
    acc_ref[...] = jnp.zeros(acc_ref.shape, F32)

    def body(c, carry):
        gv = jnp.dot(h_ref[...], wup_ref[c], preferred_element_type=F32)
        y = _conv3(gv[:, :fc], cw_ref[c], ts) + cb_ref[c]
        act = (_silu(y) * gv[HALO:HALO + ts, fc:]).astype(BF16)
        acc_ref[...] += jnp.dot(act, wdn_ref[c], preferred_element_type=F32)
        return carry
    lax.fori_loop(0, wup_ref.shape[0], body, 0)
    o_ref[0] = xm_ref[0] + g_ref[0] * acc_ref[...]


def _ffn_call(x, sh, sc, g, ng, wup, cw, cb, wdn):
    b, s, d = x.shape
    ts = _tile(s, 512)
    full = lambda a: pl.BlockSpec(a.shape, lambda bi, i: (0,) * a.ndim)
    vec = pl.BlockSpec((1, 1, d), lambda bi, i: (bi, 0, 0))
    return pl.pallas_call(
        _ffn_kernel, grid=(b, s // ts),
        in_specs=_halo_specs(ts, s, d) + [vec, vec, vec, full(ng), full(wup), full(cw), full(cb), full(wdn)],
        out_specs=pl.BlockSpec((1, ts, d), lambda bi, i: (bi, i, 0)),
        out_shape=jax.ShapeDtypeStruct((b, s, d), F32),
        scratch_shapes=[pltpu.VMEM((ts, d), F32), pltpu.VMEM((ts + 2 * HALO, d), BF16)],
        compiler_params=_params(("parallel", "arbitrary")), name="ffn",
    )(x, x, x, sh, sc, g, ng, wup, cw, cb, wdn)


def _odd_mix_kernel(xm_ref, xp_ref, xn_ref, sh_ref, sc_ref, g_ref, ng_ref, win_ref, cw_ref, cb_ref, wout_ref,
                    o_ref, acc_ref, h_ref):
    ts = xm_ref.shape[1]
    mc = wout_ref.shape[1]
    h_ref[...] = _halo_hidden(xm_ref, xp_ref, xn_ref, sh_ref, sc_ref, ng_ref)
    acc_ref[...] = jnp.zeros(acc_ref.shape, F32)

    def body(c, carry):
        t = jnp.dot(h_ref[...], win_ref[c], preferred_element_type=F32)
        conv = _conv3(t[:, mc:2 * mc] * t[:, 2 * mc:], cw_ref[c], ts) + cb_ref[c]
        y = (t[HALO:HALO + ts, :mc] * conv).astype(BF16)
        acc_ref[...] += jnp.dot(y, wout_ref[c], preferred_element_type=F32)
        return carry
    lax.fori_loop(0, win_ref.shape[0], body, 0)
    o_ref[0] = xm_ref[0] + g_ref[0] * acc_ref[...]


def _odd_mix_call(x, sh, sc, g, ng, win, cw, cb, wout):
    b, s, d = x.shape
    ts = _tile(s, 512)
    full = lambda a: pl.BlockSpec(a.shape, lambda bi, i: (0,) * a.ndim)
    vec = pl.BlockSpec((1, 1, d), lambda bi, i: (bi, 0, 0))
    return pl.pallas_call(
        _odd_mix_kernel, grid=(b, s // ts),
        in_specs=_halo_specs(ts, s, d) + [vec, vec, vec, full(ng), full(win), full(cw), full(cb), full(wout)],
        out_specs=pl.BlockSpec((1, ts, d), lambda bi, i: (bi, i, 0)),
        out_shape=jax.ShapeDtypeStruct((b, s, d), F32),
        scratch_shapes=[pltpu.VMEM((ts, d), F32), pltpu.VMEM((ts + 2 * HALO, d), BF16)],
        compiler_params=_params(("parallel", "arbitrary")), name="odd_mix",
    )(x, x, x, sh, sc, g, ng, win, cw, cb, wout)


def _rope_tables(n_tokens):
    half = QK_ROPE // 2
    inv = ROPE_THETA ** (-jnp.arange(0, half, 2, dtype=F32) / half)
    pos = jnp.arange(n_tokens)
    q8 = half // 2
    cos = jnp.ones((n_tokens, LANES), F32)
    sina = jnp.zeros((n_tokens, LANES), F32)
    sinb = jnp.zeros((n_tokens, LANES), F32)
    for k, p in enumerate((pos // GRID_W, pos % GRID_W)):
        ang = p.astype(F32)[:, None] * inv[None, :]
        c, sn = jnp.cos(ang), jnp.sin(ang)
        o = QK_NOPE + k * half
        cos = cos.at[:, o:o + q8].set(c).at[:, o + q8:o + half].set(c)
        sina = sina.at[:, o:o + q8].set(-sn)
        sinb = sinb.at[:, o + q8:o + half].set(sn)
    return cos, sina, sinb


def _pad_heads(w, heads, width):
    k = w.shape[0]
    w = w.reshape(k, heads, width)
    return jnp.pad(w, ((0, 0), (0, 0), (0, LANES - width))).reshape(k, heads * LANES)


def _chunk_cols(w, parts, chunk):
    d = w.shape[0]
    n = w.shape[1] // parts
    w = w.reshape(d, parts, n // chunk, chunk)
    return jnp.transpose(w, (2, 0, 1, 3)).reshape(n // chunk, d, parts * chunk)


def _chunk_vec(v, chunk):
    k, n = v.shape
    return jnp.transpose(v.reshape(k, n // chunk, chunk), (1, 0, 2))


def kernel(x, c, ctx, c_ctx, ada_w, ada_b, norm_mix_g, norm_ffn_g, ffn_w_up, ffn_conv_w, ffn_conv_b, ffn_w_down,
           ev_w_in, ev_conv_w, ev_conv_b, ev_ln_g, ev_ln_b, ev_qa_norm_g, ev_w_uq, ev_kva_norm_g, ev_w_ukv,
           ev_q_norm_g, ev_k_norm_g, ev_w_out, od_w_in, od_conv_w, od_conv_b, od_w_out):
    b, s, d = x.shape
    n_ctx = ctx.shape[1]
    depth = ada_w.shape[0]
    assert depth == 2, "kernel implements the two-layer (even, odd) block"
    conv_ch = ev_conv_w.shape[2]
    q_lora = ev_qa_norm_g.shape[1]
    kv_lora = ev_kva_norm_g.shape[1]
    heads = ev_w_uq.shape[2] // QK_DIM
    d_mix = od_conv_w.shape[2]
    ffn_dim = ffn_conv_w.shape[2]
    assert s % GRID_W == 0 and s % LANES == 0 and n_ctx % LANES == 0 and heads % 2 == 0
    assert ev_w_in.shape[2] == 2 * conv_ch + q_lora + kv_lora + QK_ROPE

    rows = -(-(b + 1) // HALO) * HALO
    cvec = jnp.concatenate([c, c_ctx[None, :], jnp.zeros((rows - b - 1, d), F32)], axis=0)
    mod = _mod_call(cvec, ada_w, ada_b)
    lat = lambda l, k: mod[l, :b, None, k * d:(k + 1) * d]
    ctxm = lambda l, k: jnp.broadcast_to(mod[l, b:b + 1, None, k * d:(k + 1) * d], (b, 1, d))

    w_in = ev_w_in[0]
    o_kr = 2 * conv_ch + q_lora + kv_lora
    kr_tile = jnp.pad(w_in[:, o_kr:], ((0, 0), (QK_NOPE, LANES - QK_DIM)))
    w1 = jnp.concatenate([w_in[:, :o_kr], kr_tile], axis=1).astype(BF16)
    wq = _pad_heads(ev_w_uq[0], heads, QK_DIM).astype(BF16)
    wkv = ev_w_ukv[0].reshape(kv_lora, heads, QK_NOPE + V_DIM)
    wk = jnp.pad(wkv[:, :, :QK_NOPE], ((0, 0), (0, 0), (0, LANES - QK_NOPE))).reshape(kv_lora, heads * LANES)
    wk = wk.astype(BF16)
    wv = wkv[:, :, QK_NOPE:].reshape(kv_lora, heads * V_DIM).astype(BF16)
    gq = jnp.pad(ev_q_norm_g[0], (0, LANES - QK_DIM))[None, :]
    gk = jnp.pad(ev_k_norm_g[0], (0, LANES - QK_DIM))[None, :]
    even_kw = dict(heads=heads, conv_ch=conv_ch, q_lora=q_lora, kv_lora=kv_lora)
    even_w = (norm_mix_g[0:1], w1, ev_qa_norm_g, wq, ev_kva_norm_g, wk, wv, gq, gk)

    cos, sina, sinb = _rope_tables(s)
    ident = (jnp.ones((n_ctx, LANES), F32), jnp.zeros((n_ctx, LANES), F32), jnp.zeros((n_ctx, LANES), F32))

    ktc, vc = _even_in_call(ctx, ctxm(0, 0), ctxm(0, 1), *even_w, *ident, with_q=False, name="even_in_ctx",
                            **even_kw)
    u, q, kt, v = _even_in_call(x, lat(0, 0), lat(0, 1), *even_w, cos, sina, sinb, with_q=True, name="even_in",
                                **even_kw)
    a = _conf_conv_call(u, ev_conv_w[0], ev_conv_b, ev_ln_g, ev_ln_b)
    att = _attn_call(q, ktc, vc, kt, v)
    w_out = ev_w_out[0].astype(BF16)
    x = _out_proj_call(x, a, att, lat(0, 2), w_out[:conv_ch], w_out[conv_ch:])

    fc = MXU_WIDTH if ffn_dim % MXU_WIDTH == 0 else LANES

    def ffn(xin, l):
        wup = _chunk_cols(ffn_w_up[l], 2, fc).astype(BF16)
        wdn = ffn_w_down[l].reshape(ffn_dim // fc, fc, d).astype(BF16)
        return _ffn_call(xin, lat(l, 3), lat(l, 4), lat(l, 5), norm_ffn_g[l:l + 1], wup,
                         _chunk_vec(ffn_conv_w[l], fc), _chunk_vec(ffn_conv_b[l:l + 1], fc), wdn)

    x = ffn(x, 0)

    mc = MXU_WIDTH if d_mix % MXU_WIDTH == 0 else LANES
    win = _chunk_cols(od_w_in[0], 3, mc).astype(BF16)
    wout = od_w_out[0].reshape(d_mix // mc, mc, d).astype(BF16)
    x = _odd_mix_call(x, lat(1, 0), lat(1, 1), lat(1, 2), norm_mix_g[1:2], win, _chunk_vec(od_conv_w[0], mc),
                      _chunk_vec(od_conv_b, mc), wout)
    return ffn(x, 1)
```

```python
import functools
import math

import jax
import jax.numpy as jnp
from jax import lax
from jax.experimental import pallas as pl
from jax.experimental.pallas import tpu as pltpu

F32 = jnp.float32
BF16 = jnp.bfloat16

GRID_W = 64
QK_NOPE = 64
QK_ROPE = 32
V_DIM = 64
QK_DIM = QK_NOPE + QK_ROPE
ROPE_THETA = 10000.0
EPS = 1e-6
SM_SCALE = QK_DIM ** -0.5
Q_SCALE = SM_SCALE * math.log2(math.e)
N_MOD = 6

LANES = 128
MXU_WIDTH = 256
VMEM_LIMIT = 60000 * 1024

ATTN_KB = 1024
HALO = 8
CONF_HALO = 16
NEG_BIG = -1e30


def _tile(n, pref):
    t = min(n, pref)
    while n % t:
        t //= 2
    return t


def _rms(x, g, n):
    ms = jnp.sum(x * x, axis=-1, keepdims=True) * (1.0 / n)
    return x * lax.rsqrt(ms + EPS) * g


def _silu(x):
    return x * jax.nn.sigmoid(x)


def _params(sem):
    return pltpu.CompilerParams(dimension_semantics=sem, vmem_limit_bytes=VMEM_LIMIT)


def _mod_kernel(c_ref, w_ref, b_ref, o_ref):
    s = _silu(c_ref[...])
    o_ref[0] = jnp.dot(s, w_ref[0], preferred_element_type=F32,
                       precision=lax.Precision.HIGHEST) + b_ref[0]


def _mod_call(cvec, ada_w, ada_b):
    depth, d, nd = ada_w.shape
    rows = cvec.shape[0]
    tn = _tile(nd, 1536)
    return pl.pallas_call(
        _mod_kernel,
        grid=(depth, nd // tn),
        in_specs=[
            pl.BlockSpec((rows, d), lambda l, j: (0, 0)),
            pl.BlockSpec((1, d, tn), lambda l, j: (l, 0, j)),
            pl.BlockSpec((1, 1, tn), lambda l, j: (l, 0, j)),
        ],
        out_specs=pl.BlockSpec((1, rows, tn), lambda l, j: (l, 0, j)),
        out_shape=jax.ShapeDtypeStruct((depth, rows, nd), F32),
        compiler_params=_params(("arbitrary", "arbitrary")),
        name="mod",
    )(cvec, ada_w, ada_b.reshape(depth, 1, nd))


def _rope(x, cos, sina, sinb):
    return x * cos + pltpu.roll(x, LANES - 8, 1) * sina + pltpu.roll(x, 8, 1) * sinb


def _even_in_kernel(x_ref, sh_ref, sc_ref, gmix_ref, w1_ref, qag_ref, wq_ref, kvag_ref, wk_ref, wv_ref,
                    gq_ref, gk_ref, cos_ref, sina_ref, sinb_ref, *out_refs, heads, conv_ch, q_lora,
                    kv_lora, with_q):
    if with_q:
        u_ref, q_ref, kt_ref, v_ref = out_refs
    else:
        kt_ref, v_ref = out_refs
    ts, d = x_ref.shape[1], x_ref.shape[2]
    x = x_ref[0]
    h = _rms(x, gmix_ref[...], d) * (1.0 + sc_ref[0]) + sh_ref[0]
    proj = jnp.dot(h.astype(BF16), w1_ref[...], preferred_element_type=F32)
    o_cq = 2 * conv_ch
    o_ckv = o_cq + q_lora
    o_kr = o_ckv + kv_lora
    cos, sina, sinb = cos_ref[...], sina_ref[...], sinb_ref[...]
    lane = lax.broadcasted_iota(jnp.int32, (1, LANES), 1)

    if with_q:
        u_ref[0] = proj[:, :conv_ch] * jax.nn.sigmoid(proj[:, conv_ch:o_cq])
        cqn = _rms(proj[:, o_cq:o_ckv], qag_ref[...], q_lora)
        qf = jnp.dot(cqn.astype(BF16), wq_ref[...], preferred_element_type=F32)
        gq = gq_ref[...]
        for hd in range(heads):
            qh = qf[:, hd * LANES:(hd + 1) * LANES]
            r = lax.rsqrt(jnp.sum(qh * qh, axis=-1, keepdims=True) * (1.0 / QK_DIM) + EPS)
            q_ref[0, hd] = (_rope(qh * r * gq, cos, sina, sinb) * Q_SCALE).astype(BF16)

    ckvn = _rms(proj[:, o_ckv:o_kr], kvag_ref[...], kv_lora).astype(BF16)
    kf = jnp.dot(ckvn, wk_ref[...], preferred_element_type=F32)
    vf = jnp.dot(ckvn, wv_ref[...], preferred_element_type=F32)
    krt = proj[:, o_kr:o_kr + LANES]
    gk = gk_ref[...]
    for hd in range(heads):
        kh = kf[:, hd * LANES:(hd + 1) * LANES] + krt
        r = lax.rsqrt(jnp.sum(kh * kh, axis=-1, keepdims=True) * (1.0 / QK_DIM) + EPS)
        kt_ref[0, hd, 0] = _rope(kh * r * gk, cos, sina, sinb).T.astype(BF16)
        one_hot = jnp.where(lane == _sum_lane(hd), 1.0, 0.0)
        v_ref[0, hd] = (vf[:, hd * LANES:(hd + 1) * LANES] + one_hot).astype(BF16)


def _sum_lane(head):
    return V_DIM if head % 2 == 0 else 0


def _even_in_call(x, sh, sc, gmix, w1, qag, wq, kvag, wk, wv, gq, gk, cos, sina, sinb, *, heads, conv_ch,
                  q_lora, kv_lora, with_q, name):
    b, s, d = x.shape
    ts = _tile(s, 512)
    kb = _tile(s, ATTN_KB)
    r = kb // ts
    kern = functools.partial(_even_in_kernel, heads=heads, conv_ch=conv_ch, q_lora=q_lora, kv_lora=kv_lora,
                             with_q=with_q)
    full = lambda a: pl.BlockSpec(a.shape, lambda bi, i: (0,) * a.ndim)
    vec = pl.BlockSpec((1, 1, d), lambda bi, i: (bi, 0, 0))
    tab = pl.BlockSpec((ts, LANES), lambda bi, i: (i, 0))
    in_specs = [pl.BlockSpec((1, ts, d), lambda bi, i: (bi, i, 0)), vec, vec, full(gmix), full(w1), full(qag),
                full(wq), full(kvag), full(wk), full(wv), full(gq), full(gk), tab, tab, tab]
    head_rows = pl.BlockSpec((1, heads, ts, LANES), lambda bi, i: (bi, 0, i, 0))
    out_specs = [pl.BlockSpec((1, heads, 1, LANES, ts), lambda bi, i: (bi, 0, i // r, 0, i % r)), head_rows]
    out_shape = [jax.ShapeDtypeStruct((b, heads, s // kb, LANES, kb), BF16),
                 jax.ShapeDtypeStruct((b, heads, s, LANES), BF16)]
    if with_q:
        out_specs = [pl.BlockSpec((1, ts, conv_ch), lambda bi, i: (bi, i, 0)), head_rows] + out_specs
        out_shape = [jax.ShapeDtypeStruct((b, s, conv_ch), F32),
                     jax.ShapeDtypeStruct((b, heads, s, LANES), BF16)] + out_shape
    return pl.pallas_call(
        kern, grid=(b, s // ts), in_specs=in_specs, out_specs=out_specs, out_shape=out_shape,
        compiler_params=_params(("parallel", "arbitrary")), name=name,
    )(x, sh, sc, gmix, w1, qag, wq, kvag, wk, wv, gq, gk, cos, sina, sinb)


def _conf_conv_kernel(um_ref, up_ref, un_ref, w_ref, b_ref, g_ref, beta_ref, o_ref, buf_ref, *, taps, rc):
    i, n = pl.program_id(1), pl.num_programs(1)
    ts, ch = um_ref.shape[1], um_ref.shape[2]
    buf_ref[0:CONF_HALO] = jnp.where(i > 0, up_ref[0], 0.0)
    buf_ref[CONF_HALO:CONF_HALO + ts] = um_ref[0]
    buf_ref[CONF_HALO + ts:] = jnp.where(i < n - 1, un_ref[0], 0.0)
    pad = (taps - 1) // 2
    for c in range(ts // rc):
        acc = jnp.zeros((rc, ch), F32)
        for k in range(taps):
            acc = acc + w_ref[k:k + 1, :] * buf_ref[pl.ds(c * rc + CONF_HALO - pad + k, rc), :]
        u = acc + b_ref[...]
        mu = jnp.mean(u, axis=-1, keepdims=True)
        xc = u - mu
        y = xc * lax.rsqrt(jnp.mean(xc * xc, axis=-1, keepdims=True) + EPS) * g_ref[...] + beta_ref[...]
        o_ref[0, c * rc:(c + 1) * rc] = _silu(y).astype(o_ref.dtype)


def _conf_conv_call(u, w, bias, g, beta):
    b, s, ch = u.shape
    taps = w.shape[0]
    ts = _tile(s, 512)
    hb = ts // CONF_HALO
    nh = s // CONF_HALO
    kern = functools.partial(_conf_conv_kernel, taps=taps, rc=_tile(ts, 64))
    full = lambda a: pl.BlockSpec(a.shape, lambda bi, i: (0,) * a.ndim)
    return pl.pallas_call(
        kern, grid=(b, s // ts),
        in_specs=[pl.BlockSpec((1, ts, ch), lambda bi, i: (bi, i, 0)),
                  pl.BlockSpec((1, CONF_HALO, ch), lambda bi, i: (bi, jnp.maximum(i * hb - 1, 0), 0)),
                  pl.BlockSpec((1, CONF_HALO, ch), lambda bi, i: (bi, jnp.minimum((i + 1) * hb, nh - 1), 0)),
                  full(w), full(bias), full(g), full(beta)],
        out_specs=pl.BlockSpec((1, ts, ch), lambda bi, i: (bi, i, 0)),
        out_shape=jax.ShapeDtypeStruct((b, s, ch), BF16),
        scratch_shapes=[pltpu.VMEM((ts + 2 * CONF_HALO, ch), F32)],
        compiler_params=_params(("parallel", "arbitrary")), name="conf_conv",
    )(u, u, u, w, bias, g, beta)


def _softmax_pv(h, s, v_blk, acc_ref, m_ref):
    nt = s.shape[1] // LANES
    tile = lambda t: s[:, t * LANES:(t + 1) * LANES]
    mx = tile(0)
    for t in range(1, nt):
        mx = jnp.maximum(mx, tile(t))
    m_old = m_ref[h]
    m_new = jnp.maximum(m_old, jnp.max(mx, axis=-1, keepdims=True))
    m_ref[h] = m_new
    pm = jnp.concatenate([jnp.exp2(tile(t) - m_new).astype(BF16) for t in range(nt)], axis=1)
    pv = jnp.dot(pm, v_blk, preferred_element_type=F32)
    acc_ref[h] = acc_ref[h] * jnp.exp2(m_old - m_new) + pv


def _attn_kernel(q_ref, ktc_ref, vc_ref, kt_ref, v_ref, o_ref, acc_ref, m_ref, s0_ref, s1_ref):
    heads, nkb, kb = kt_ref.shape[1], kt_ref.shape[2], kt_ref.shape[4]
    n_items = heads * nkb
    s_refs = (s0_ref, s1_ref)
    acc_ref[...] = jnp.zeros(acc_ref.shape, F32)
    m_ref[...] = jnp.full(m_ref.shape, NEG_BIG, F32)

    def ctx_body(h, carry):
        s = jnp.dot(q_ref[0, h], ktc_ref[0, h, 0], preferred_element_type=F32)
        _softmax_pv(h, s, vc_ref[0, h], acc_ref, m_ref)
        return carry
    lax.fori_loop(0, heads, ctx_body, 0)

    def scores(t, slot):
        s_refs[slot][...] = jnp.dot(q_ref[0, t // nkb], kt_ref[0, t // nkb, t % nkb],
                                    preferred_element_type=F32)

    def fold(t, slot):
        rows = pl.ds(pl.multiple_of((t % nkb) * kb, kb), kb)
        _softmax_pv(t // nkb, s_refs[slot][...], v_ref[0, t // nkb, rows, :], acc_ref, m_ref)

    scores(0, 0)

    def body(u, carry):
        scores(2 * u + 1, 1)
        fold(2 * u, 0)
        scores(2 * u + 2, 0)
        fold(2 * u + 1, 1)
        return carry
    lax.fori_loop(0, n_items // 2 - 1, body, 0)
    scores(n_items - 1, 1)
    fold(n_items - 2, 0)
    fold(n_items - 1, 1)

    lane = lax.broadcasted_iota(jnp.int32, (1, LANES), 1)
    for p in range(heads // 2):
        even, odd = acc_ref[2 * p], acc_ref[2 * p + 1]
        l_even = even[:, _sum_lane(0):_sum_lane(0) + 1]
        l_odd = odd[:, _sum_lane(1):_sum_lane(1) + 1]
        o_ref[0, :, p * LANES:(p + 1) * LANES] = jnp.where(lane < V_DIM, even / l_even, odd / l_odd).astype(
            o_ref.dtype)


def _attn_call(q, ktc, vc, kt, v):
    b, heads, s, _ = q.shape
    tq = _tile(s, 512)
    kb = kt.shape[4]
    resident = lambda a: pl.BlockSpec((1,) + a.shape[1:], lambda bi, i: (bi,) + (0,) * (a.ndim - 1),
                                      pipeline_mode=pl.Buffered(1))
    return pl.pallas_call(
        _attn_kernel, grid=(b, s // tq),
        in_specs=[pl.BlockSpec((1, heads, tq, LANES), lambda bi, i: (bi, 0, i, 0)),
                  resident(ktc), resident(vc), resident(kt), resident(v)],
        out_specs=pl.BlockSpec((1, tq, heads * V_DIM), lambda bi, i: (bi, i, 0)),
        out_shape=jax.ShapeDtypeStruct((b, s, heads * V_DIM), BF16),
        scratch_shapes=[pltpu.VMEM((heads, tq, LANES), F32), pltpu.VMEM((heads, tq, LANES), F32),
                        pltpu.VMEM((tq, kb), F32), pltpu.VMEM((tq, kb), F32)],
        compiler_params=_params(("parallel", "arbitrary")), name="attn",
    )(q, ktc, vc, kt, v)


def _out_proj_kernel(x_ref, a_ref, t_ref, g_ref, wa_ref, wt_ref, o_ref):
    y = jnp.dot(a_ref[0], wa_ref[...], preferred_element_type=F32)
    y = y + jnp.dot(t_ref[0], wt_ref[...], preferred_element_type=F32)
    o_ref[0] = x_ref[0] + g_ref[0] * y


def _out_proj_call(x, a, att, g, wa, wt):
    b, s, d = x.shape
    ts = _tile(s, 1024)
    row = lambda c: pl.BlockSpec((1, ts, c), lambda bi, i: (bi, i, 0))
    full = lambda arr: pl.BlockSpec(arr.shape, lambda bi, i: (0,) * arr.ndim)
    return pl.pallas_call(
        _out_proj_kernel, grid=(b, s // ts),
        in_specs=[row(d), row(a.shape[2]), row(att.shape[2]),
                  pl.BlockSpec((1, 1, d), lambda bi, i: (bi, 0, 0)), full(wa), full(wt)],
        out_specs=row(d), out_shape=jax.ShapeDtypeStruct((b, s, d), F32),
        compiler_params=_params(("parallel", "arbitrary")), name="out_proj",
    )(x, a, att, g, wa, wt)


def _halo_hidden(xm_ref, xp_ref, xn_ref, sh_ref, sc_ref, ng_ref):
    i, n = pl.program_id(1), pl.num_programs(1)
    ts, d = xm_ref.shape[1], xm_ref.shape[2]
    xe = jnp.concatenate([xp_ref[0], xm_ref[0], xn_ref[0]], axis=0)
    h = _rms(xe, ng_ref[...], d) * (1.0 + sc_ref[0]) + sh_ref[0]
    row = lax.broadcasted_iota(jnp.int32, (ts + 2 * HALO, 1), 0)
    lo = jnp.where(i > 0, 0, HALO)
    hi = jnp.where(i < n - 1, ts + 2 * HALO, ts + HALO)
    valid = jnp.logical_and(row >= lo, row < hi)
    return jnp.where(valid, h, 0.0).astype(BF16)


def _conv3(z, w, ts):
    rows = z.shape[0]
    y = w[0:1] * pltpu.roll(z, 1, 0) + w[1:2] * z + w[2:3] * pltpu.roll(z, rows - 1, 0)
    return y[HALO:HALO + ts]


def _halo_specs(ts, s, d):
    hb, nh = ts // HALO, s // HALO
    return [pl.BlockSpec((1, ts, d), lambda bi, i: (bi, i, 0)),
            pl.BlockSpec((1, HALO, d), lambda bi, i: (bi, jnp.maximum(i * hb - 1, 0), 0)),
            pl.BlockSpec((1, HALO, d), lambda bi, i: (bi, jnp.minimum((i + 1) * hb, nh - 1), 0))]


def _ffn_kernel(xm_ref, xp_ref, xn_ref, sh_ref, sc_ref, g_ref, ng_ref, wup_ref, cw_ref, cb_ref, wdn_ref,
                o_ref, acc_ref, h_ref):
    ts = xm_ref.shape[1]
    fc = wdn_ref.shape[1]
    h_ref[...] = _halo_hidden(xm_ref, xp_ref, xn_ref, sh_ref, sc_ref, ng_ref)
    acc_ref[...] = jnp.zeros(acc_ref.shape, F32)

    def body(c, carry):
        gv = jnp.dot(h_ref[...], wup_ref[c], preferred_element_type=F32)
        y = _conv3(gv[:, :fc], cw_ref[c], ts) + cb_ref[c]
        act = (_silu(y) * gv[HALO:HALO + ts, fc:]).astype(BF16)
        acc_ref[...] += jnp.dot(act, wdn_ref[c], preferred_element_type=F32)
        return carry
    lax.fori_loop(0, wup_ref.shape[0], body, 0)
    o_ref[0] = xm_ref[0] + g_ref[0] * acc_ref[...]


def _ffn_call(x, sh, sc, g, ng, wup, cw, cb, wdn):
    b, s, d = x.shape
    ts = _tile(s, 512)
    full = lambda a: pl.BlockSpec(a.shape, lambda bi, i: (0,) * a.ndim)
    vec = pl.BlockSpec((1, 1, d), lambda bi, i: (bi, 0, 0))
    return pl.pallas_call(
        _ffn_kernel, grid=(b, s // ts),
        in_specs=_halo_specs(ts, s, d) + [vec, vec, vec, full(ng), full(wup), full(cw), full(cb), full(wdn)],
        out_specs=pl.BlockSpec((1, ts, d), lambda bi, i: (bi, i, 0)),
        out_shape=jax.ShapeDtypeStruct((b, s, d), F32),
        scratch_shapes=[pltpu.VMEM((ts, d), F32), pltpu.VMEM((ts + 2 * HALO, d), BF16)],
        compiler_params=_params(("parallel", "arbitrary")), name="ffn",
    )(x, x, x, sh, sc, g, ng, wup, cw, cb, wdn)


def _odd_mix_kernel(xm_ref, xp_ref, xn_ref, sh_ref, sc_ref, g_ref, ng_ref, win_ref, cw_ref, cb_ref, wout_ref,
                    o_ref, acc_ref, h_ref):
    ts = xm_ref.shape[1]
    mc = wout_ref.shape[1]
    h_ref[...] = _halo_hidden(xm_ref, xp_ref, xn_ref, sh_ref, sc_ref, ng_ref)
    acc_ref[...] = jnp.zeros(acc_ref.shape, F32)

    def body(c, carry):
        t = jnp.dot(h_ref[...], win_ref[c], preferred_element_type=F32)
        conv = _conv3(t[:, mc:2 * mc] * t[:, 2 * mc:], cw_ref[c], ts) + cb_ref[c]
        y = (t[HALO:HALO + ts, :mc] * conv).astype(BF16)
        acc_ref[...] += jnp.dot(y, wout_ref[c], preferred_element_type=F32)
        return carry
    lax.fori_loop(0, win_ref.shape[0], body, 0)
    o_ref[0] = xm_ref[0] + g_ref[0] * acc_ref[...]


def _odd_mix_call(x, sh, sc, g, ng, win, cw, cb, wout):
    b, s, d = x.shape
    ts = _tile(s, 512)
    full = lambda a: pl.BlockSpec(a.shape, lambda bi, i: (0,) * a.ndim)
    vec = pl.BlockSpec((1, 1, d), lambda bi, i: (bi, 0, 0))
    return pl.pallas_call(
        _odd_mix_kernel, grid=(b, s // ts),
        in_specs=_halo_specs(ts, s, d) + [vec, vec, vec, full(ng), full(win), full(cw), full(cb), full(wout)],
        out_specs=pl.BlockSpec((1, ts, d), lambda bi, i: (bi, i, 0)),
        out_shape=jax.ShapeDtypeStruct((b, s, d), F32),
        scratch_shapes=[pltpu.VMEM((ts, d), F32), pltpu.VMEM((ts + 2 * HALO, d), BF16)],
        compiler_params=_params(("parallel", "arbitrary")), name="odd_mix",
    )(x, x, x, sh, sc, g, ng, win, cw, cb, wout)


def _rope_tables(n_tokens):
    half = QK_ROPE // 2
    inv = ROPE_THETA ** (-jnp.arange(0, half, 2, dtype=F32) / half)
    pos = jnp.arange(n_tokens)
    q8 = half // 2
    cos = jnp.ones((n_tokens, LANES), F32)
    sina = jnp.zeros((n_tokens, LANES), F32)
    sinb = jnp.zeros((n_tokens, LANES), F32)
    for k, p in enumerate((pos // GRID_W, pos % GRID_W)):
        ang = p.astype(F32)[:, None] * inv[None, :]
        c, sn = jnp.cos(ang), jnp.sin(ang)
        o = QK_NOPE + k * half
        cos = cos.at[:, o:o + q8].set(c).at[:, o + q8:o + half].set(c)
        sina = sina.at[:, o:o + q8].set(-sn)
        sinb = sinb.at[:, o + q8:o + half].set(sn)
    return cos, sina, sinb


def _pad_heads(w, heads, width):
    k = w.shape[0]
    w = w.reshape(k, heads, width)
    return jnp.pad(w, ((0, 0), (0, 0), (0, LANES - width))).reshape(k, heads * LANES)


def _chunk_cols(w, parts, chunk):
    d = w.shape[0]
    n = w.shape[1] // parts
    w = w.reshape(d, parts, n // chunk, chunk)
    return jnp.transpose(w, (2, 0, 1, 3)).reshape(n // chunk, d, parts * chunk)


def _chunk_vec(v, chunk):
    k, n = v.shape
    return jnp.transpose(v.reshape(k, n // chunk, chunk), (1, 0, 2))


def kernel(x, c, ctx, c_ctx, ada_w, ada_b, norm_mix_g, norm_ffn_g, ffn_w_up, ffn_conv_w, ffn_conv_b, ffn_w_down,
           ev_w_in, ev_conv_w, ev_conv_b, ev_ln_g, ev_ln_b, ev_qa_norm_g, ev_w_uq, ev_kva_norm_g, ev_w_ukv,
           ev_q_norm_g, ev_k_norm_g, ev_w_out, od_w_in, od_conv_w, od_conv_b, od_w_out):
    b, s, d = x.shape
    n_ctx = ctx.shape[1]
    depth = ada_w.shape[0]
    assert depth == 2, "kernel implements the two-layer (even, odd) block"
    conv_ch = ev_conv_w.shape[2]
    q_lora = ev_qa_norm_g.shape[1]
    kv_lora = ev_kva_norm_g.shape[1]
    heads = ev_w_uq.shape[2] // QK_DIM
    d_mix = od_conv_w.shape[2]
    ffn_dim = ffn_conv_w.shape[2]
    assert s % GRID_W == 0 and s % LANES == 0 and n_ctx % LANES == 0 and heads % 2 == 0
    assert ev_w_in.shape[2] == 2 * conv_ch + q_lora + kv_lora + QK_ROPE

    rows = -(-(b + 1) // HALO) * HALO
    cvec = jnp.concatenate([c, c_ctx[None, :], jnp.zeros((rows - b - 1, d), F32)], axis=0)
    mod = _mod_call(cvec, ada_w, ada_b)
    lat = lambda l, k: mod[l, :b, None, k * d:(k + 1) * d]
    ctxm = lambda l, k: jnp.broadcast_to(mod[l, b:b + 1, None, k * d:(k + 1) * d], (b, 1, d))

    w_in = ev_w_in[0]
    o_kr = 2 * conv_ch + q_lora + kv_lora
    kr_tile = jnp.pad(w_in[:, o_kr:], ((0, 0), (QK_NOPE, LANES - QK_DIM)))
    w1 = jnp.concatenate([w_in[:, :o_kr], kr_tile], axis=1).astype(BF16)
    wq = _pad_heads(ev_w_uq[0], heads, QK_DIM).astype(BF16)
    wkv = ev_w_ukv[0].reshape(kv_lora, heads, QK_NOPE + V_DIM)
    wk = jnp.pad(wkv[:, :, :QK_NOPE], ((0, 0), (0, 0), (0, LANES - QK_NOPE))).reshape(kv_lora, heads * LANES)
    wk = wk.astype(BF16)
    wv3 = wkv[:, :, QK_NOPE:]
    zv = jnp.zeros_like(wv3)
    even_head = (jnp.arange(heads) % 2 == 0)[None, :, None]
    wv = jnp.where(even_head, jnp.concatenate([wv3, zv], -1), jnp.concatenate([zv, wv3], -1))
    wv = wv.reshape(kv_lora, heads * LANES).astype(BF16)
    gq = jnp.pad(ev_q_norm_g[0], (0, LANES - QK_DIM))[None, :]
    gk = jnp.pad(ev_k_norm_g[0], (0, LANES - QK_DIM))[None, :]
    even_kw = dict(heads=heads, conv_ch=conv_ch, q_lora=q_lora, kv_lora=kv_lora)
    even_w = (norm_mix_g[0:1], w1, ev_qa_norm_g, wq, ev_kva_norm_g, wk, wv, gq, gk)

    cos, sina, sinb = _rope_tables(s)
    ident = (jnp.ones((n_ctx, LANES), F32), jnp.zeros((n_ctx, LANES), F32), jnp.zeros((n_ctx, LANES), F32))

    ktc, vc = _even_in_call(ctx, ctxm(0, 0), ctxm(0, 1), *even_w, *ident, with_q=False, name="even_in_ctx",
                            **even_kw)
    u, q, kt, v = _even_in_call(x, lat(0, 0), lat(0, 1), *even_w, cos, sina, sinb, with_q=True, name="even_in",
                                **even_kw)
    a = _conf_conv_call(u, ev_conv_w[0], ev_conv_b, ev_ln_g, ev_ln_b)
    att = _attn_call(q, ktc, vc, kt, v)
    w_out = ev_w_out[0].astype(BF16)
    x = _out_proj_call(x, a, att, lat(0, 2), w_out[:conv_ch], w_out[conv_ch:])

    fc = MXU_WIDTH if ffn_dim % MXU_WIDTH == 0 else LANES

    def ffn(xin, l):
        wup = _chunk_cols(ffn_w_up[l], 2, fc).astype(BF16)
        wdn = ffn_w_down[l].reshape(ffn_dim // fc, fc, d).astype(BF16)
        return _ffn_call(xin, lat(l, 3), lat(l, 4), lat(l, 5), norm_ffn_g[l:l + 1], wup,
                         _chunk_vec(ffn_conv_w[l], fc), _chunk_vec(ffn_conv_b[l:l + 1], fc), wdn)

    x = ffn(x, 0)

    mc = MXU_WIDTH if d_mix % MXU_WIDTH == 0 else LANES
    win = _chunk_cols(od_w_in[0], 3, mc).astype(BF16)
    wout = od_w_out[0].reshape(d_mix // mc, mc, d).astype(BF16)
    x = _odd_mix_call(x, lat(1, 0), lat(1, 1), lat(1, 2), norm_mix_g[1:2], win, _chunk_vec(od_conv_w[0], mc),
                      _chunk_vec(od_conv_b, mc), wout)
    return ffn(x, 1)
```

```python
import functools
import math

import jax
import jax.numpy as jnp
from jax import lax
from jax.experimental import pallas as pl
from jax.experimental.pallas import tpu as pltpu

F32 = jnp.float32
BF16 = jnp.bfloat16

GRID_W = 64
QK_NOPE = 64
QK_ROPE = 32
V_DIM = 64
QK_DIM = QK_NOPE + QK_ROPE
ROPE_THETA = 10000.0
EPS = 1e-6
SM_SCALE = QK_DIM ** -0.5
Q_SCALE = SM_SCALE * math.log2(math.e)
N_MOD = 6

LANES = 128
SUBLANES = 8
MXU_WIDTH = 256
VMEM_LIMIT = 60000 * 1024

ATTN_KB = 1024
HALO = SUBLANES
CONF_HALO = 16
NEG_BIG = -1e30


def _tile(n, pref):
    t = min(n, pref)
    while n % t:
        t //= 2
    return t


def _rms(x, g, n):
    ms = jnp.sum(x * x, axis=-1, keepdims=True) * (1.0 / n)
    return x * lax.rsqrt(ms + EPS) * g


def _silu(x):
    return x * jax.nn.sigmoid(x)


def _params(sem):
    return pltpu.CompilerParams(dimension_semantics=sem, vmem_limit_bytes=VMEM_LIMIT)


def _software_pipeline(n, produce, consume, unroll=False):
    produce(0, 0)
    if unroll:
        for t in range(n):
            if t + 1 < n:
                produce(t + 1, (t + 1) % 2)
            consume(t, t % 2)
        return

    def body(u, carry):
        produce(2 * u + 1, 1)
        consume(2 * u, 0)
        produce(2 * u + 2, 0)
        consume(2 * u + 1, 1)
        return carry
    lax.fori_loop(0, (n - 1) // 2, body, 0)
    if n % 2 == 0:
        produce(n - 1, 1)
        consume(n - 2, 0)
        consume(n - 1, 1)
    else:
        consume(n - 1, 0)


def _mod_kernel(c_ref, w_ref, b_ref, o_ref):
    s = _silu(c_ref[...])
    o_ref[0] = jnp.dot(s, w_ref[0], preferred_element_type=F32,
                       precision=lax.Precision.HIGHEST) + b_ref[0]


def _mod_call(cvec, ada_w, ada_b):
    depth, d, nd = ada_w.shape
    rows = cvec.shape[0]
    tn = _tile(nd, 1536)
    return pl.pallas_call(
        _mod_kernel,
        grid=(depth, nd // tn),
        in_specs=[
            pl.BlockSpec((rows, d), lambda l, j: (0, 0)),
            pl.BlockSpec((1, d, tn), lambda l, j: (l, 0, j)),
            pl.BlockSpec((1, 1, tn), lambda l, j: (l, 0, j)),
        ],
        out_specs=pl.BlockSpec((1, rows, tn), lambda l, j: (l, 0, j)),
        out_shape=jax.ShapeDtypeStruct((depth, rows, nd), F32),
        compiler_params=_params(("arbitrary", "arbitrary")),
        name="mod",
    )(cvec, ada_w, ada_b.reshape(depth, 1, nd))


def _even_in_kernel(x_ref, sh_ref, sc_ref, gmix_ref, w1_ref, qag_ref, wq_ref, kvag_ref, wk_ref, wv_ref,
                    gq_ref, gqp_ref, gk_ref, gkp_ref, cos_ref, ssin_ref, *out_refs, heads, conv_ch, q_lora,
                    kv_lora, with_q):
    if with_q:
        u_ref, q_ref, kt_ref, v_ref = out_refs
    else:
        kt_ref, v_ref = out_refs
    ts, d = x_ref.shape[1], x_ref.shape[2]
    x = x_ref[0]
    h = _rms(x, gmix_ref[...], d) * (1.0 + sc_ref[0]) + sh_ref[0]
    proj = jnp.dot(h.astype(BF16), w1_ref[...], preferred_element_type=F32)
    o_cq = 2 * conv_ch
    o_ckv = o_cq + q_lora
    o_kr = o_ckv + kv_lora
    cos, ssin = cos_ref[...], ssin_ref[...]
    lane = lax.broadcasted_iota(jnp.int32, (1, LANES), 1)

    if with_q:
        u_ref[0] = proj[:, :conv_ch] * jax.nn.sigmoid(proj[:, conv_ch:o_cq])
        cqn = _rms(proj[:, o_cq:o_ckv], qag_ref[...], q_lora)
        qf = jnp.dot(cqn.astype(BF16), wq_ref[...], preferred_element_type=F32)
        q_cos = gq_ref[...] * cos
        q_sin = gqp_ref[...] * ssin
        for hd in range(heads):
            qh = qf[:, hd * LANES:(hd + 1) * LANES]
            qp = qf[:, (heads + hd) * LANES:(heads + hd + 1) * LANES]
            r = lax.rsqrt(jnp.sum(qh * qh, axis=-1, keepdims=True) * (1.0 / QK_DIM) + EPS)
            q_ref[0, hd] = ((qh * q_cos + qp * q_sin) * (r * Q_SCALE)).astype(BF16)

    ckvn = _rms(proj[:, o_ckv:o_kr], kvag_ref[...], kv_lora).astype(BF16)
    kf = jnp.dot(ckvn, wk_ref[...], preferred_element_type=F32)
    vf = jnp.dot(ckvn, wv_ref[...], preferred_element_type=F32)
    kr = proj[:, o_kr:o_kr + LANES]
    krp = proj[:, o_kr + LANES:o_kr + 2 * LANES]
    gk = gk_ref[...]
    k_rot = kr * (gk * cos) + krp * (gkp_ref[...] * ssin)
    ss_rot = jnp.sum(kr * kr, axis=-1, keepdims=True)
    for hd in range(heads):
        kn = kf[:, hd * LANES:(hd + 1) * LANES]
        r = lax.rsqrt((jnp.sum(kn * kn, axis=-1, keepdims=True) + ss_rot) * (1.0 / QK_DIM) + EPS)
        kt_ref[0, hd, 0] = ((kn * gk + k_rot) * r).T.astype(BF16)
        one_hot = jnp.where(lane == _sum_lane(hd), 1.0, 0.0)
        v_ref[0, hd] = (vf[:, hd * LANES:(hd + 1) * LANES] + one_hot).astype(BF16)


def _sum_lane(head):
    return V_DIM if head % 2 == 0 else 0


def _even_in_call(x, sh, sc, gmix, w1, qag, wq, kvag, wk, wv, gq, gqp, gk, gkp, cos, ssin, *, heads, conv_ch,
                  q_lora, kv_lora, with_q, name):
    b, s, d = x.shape
    ts = _tile(s, 512)
    kb = _tile(s, ATTN_KB)
    r = kb // ts
    kern = functools.partial(_even_in_kernel, heads=heads, conv_ch=conv_ch, q_lora=q_lora, kv_lora=kv_lora,
                             with_q=with_q)
    full = lambda a: pl.BlockSpec(a.shape, lambda bi, i: (0,) * a.ndim)
    vec = pl.BlockSpec((1, 1, d), lambda bi, i: (bi, 0, 0))
    tab = pl.BlockSpec((ts, LANES), lambda bi, i: (i, 0))
    in_specs = [pl.BlockSpec((1, ts, d), lambda bi, i: (bi, i, 0)), vec, vec, full(gmix), full(w1), full(qag),
                full(wq), full(kvag), full(wk), full(wv), full(gq), full(gqp), full(gk), full(gkp), tab, tab]
    head_rows = pl.BlockSpec((1, heads, ts, LANES), lambda bi, i: (bi, 0, i, 0))
    out_specs = [pl.BlockSpec((1, heads, 1, LANES, ts), lambda bi, i: (bi, 0, i // r, 0, i % r)), head_rows]
    out_shape = [jax.ShapeDtypeStruct((b, heads, s // kb, LANES, kb), BF16),
                 jax.ShapeDtypeStruct((b, heads, s, LANES), BF16)]
    if with_q:
        out_specs = [pl.BlockSpec((1, ts, conv_ch), lambda bi, i: (bi, i, 0)), head_rows] + out_specs
        out_shape = [jax.ShapeDtypeStruct((b, s, conv_ch), F32),
                     jax.ShapeDtypeStruct((b, heads, s, LANES), BF16)] + out_shape
    return pl.pallas_call(
        kern, grid=(b, s // ts), in_specs=in_specs, out_specs=out_specs, out_shape=out_shape,
        compiler_params=_params(("parallel", "arbitrary")), name=name,
    )(x, sh, sc, gmix, w1, qag, wq, kvag, wk, wv, gq, gqp, gk, gkp, cos, ssin)


def _conf_conv_kernel(um_ref, up_ref, un_ref, w_ref, b_ref, g_ref, beta_ref, o_ref, buf_ref, shift_ref, *,
                      taps, rc):
    i, n = pl.program_id(1), pl.num_programs(1)
    ts, ch = um_ref.shape[1], um_ref.shape[2]
    buf_ref[0:CONF_HALO] = jnp.where(i > 0, up_ref[0], 0.0)
    buf_ref[CONF_HALO:CONF_HALO + ts] = um_ref[0]
    buf_ref[CONF_HALO + ts:] = jnp.where(i < n - 1, un_ref[0], 0.0)
    n_rows = shift_ref.shape[1]
    for j in range(1, SUBLANES):
        shift_ref[j - 1] = buf_ref[pl.ds(j, n_rows), :]
    first = CONF_HALO - (taps - 1) // 2
    for c in range(ts // rc):
        acc = jnp.zeros((rc, ch), F32)
        for k in range(taps):
            q, j = divmod(first + k, SUBLANES)
            rows = pl.ds(c * rc + q * SUBLANES, rc)
            tap = buf_ref[rows, :] if j == 0 else shift_ref[j - 1, rows, :]
            acc = acc + w_ref[k:k + 1, :] * tap
        u = acc + b_ref[...]
        mu = jnp.mean(u, axis=-1, keepdims=True)
        xc = u - mu
        y = xc * lax.rsqrt(jnp.mean(xc * xc, axis=-1, keepdims=True) + EPS) * g_ref[...] + beta_ref[...]
        o_ref[0, c * rc:(c + 1) * rc] = _silu(y).astype(o_ref.dtype)


def _conf_conv_call(u, w, bias, g, beta):
    b, s, ch = u.shape
    taps = w.shape[0]
    assert (taps - 1) // 2 <= CONF_HALO
    ts = _tile(s, 512)
    hb = ts // CONF_HALO
    nh = s // CONF_HALO
    kern = functools.partial(_conf_conv_kernel, taps=taps, rc=_tile(ts, 64))
    full = lambda a: pl.BlockSpec(a.shape, lambda bi, i: (0,) * a.ndim)
    return pl.pallas_call(
        kern, grid=(b, s // ts),
        in_specs=[pl.BlockSpec((1, ts, ch), lambda bi, i: (bi, i, 0)),
                  pl.BlockSpec((1, CONF_HALO, ch), lambda bi, i: (bi, jnp.maximum(i * hb - 1, 0), 0)),
                  pl.BlockSpec((1, CONF_HALO, ch), lambda bi, i: (bi, jnp.minimum((i + 1) * hb, nh - 1), 0)),
                  full(w), full(bias), full(g), full(beta)],
        out_specs=pl.BlockSpec((1, ts, ch), lambda bi, i: (bi, i, 0)),
        out_shape=jax.ShapeDtypeStruct((b, s, ch), BF16),
        scratch_shapes=[pltpu.VMEM((ts + 2 * CONF_HALO, ch), F32),
                        pltpu.VMEM((SUBLANES - 1, ts + 2 * CONF_HALO - SUBLANES, ch), F32)],
        compiler_params=_params(("parallel", "arbitrary")), name="conf_conv",
    )(u, u, u, w, bias, g, beta)


def _softmax_pv(h, s, v_blk, acc_ref, m_ref):
    nt = s.shape[1] // LANES
    tile = lambda t: s[:, t * LANES:(t + 1) * LANES]
    mx = tile(0)
    for t in range(1, nt):
        mx = jnp.maximum(mx, tile(t))
    m_old = m_ref[h]
    m_new = jnp.maximum(m_old, jnp.max(mx, axis=-1, keepdims=True))
    m_ref[h] = m_new
    pm = jnp.concatenate([jnp.exp2(tile(t) - m_new).astype(BF16) for t in range(nt)], axis=1)
    pv = jnp.dot(pm, v_blk, preferred_element_type=F32)
    acc_ref[h] = acc_ref[h] * jnp.exp2(m_old - m_new) + pv


def _attn_kernel(q_ref, ktc_ref, vc_ref, kt_ref, v_ref, o_ref, acc_ref, m_ref, s0_ref, s1_ref, c0_ref, c1_ref):
    heads, nkb, kb = kt_ref.shape[1], kt_ref.shape[2], kt_ref.shape[4]
    n_items = heads * nkb
    s_refs = (s0_ref, s1_ref)
    acc_ref[...] = jnp.zeros(acc_ref.shape, F32)
    m_ref[...] = jnp.full(m_ref.shape, NEG_BIG, F32)

    c_refs = (c0_ref, c1_ref)

    def ctx_scores(h, slot):
        c_refs[slot][...] = jnp.dot(q_ref[0, h], ktc_ref[0, h, 0], preferred_element_type=F32)

    def ctx_fold(h, slot):
        _softmax_pv(h, c_refs[slot][...], vc_ref[0, h], acc_ref, m_ref)

    _software_pipeline(heads, ctx_scores, ctx_fold)

    def scores(t, slot):
        s_refs[slot][...] = jnp.dot(q_ref[0, t // nkb], kt_ref[0, t // nkb, t % nkb],
                                    preferred_element_type=F32)

    def fold(t, slot):
        rows = pl.ds(pl.multiple_of((t % nkb) * kb, kb), kb)
        _softmax_pv(t // nkb, s_refs[slot][...], v_ref[0, t // nkb, rows, :], acc_ref, m_ref)

    _software_pipeline(n_items, scores, fold)

    lane = lax.broadcasted_iota(jnp.int32, (1, LANES), 1)
    for p in range(heads // 2):
        even, odd = acc_ref[2 * p], acc_ref[2 * p + 1]
        l_even = even[:, _sum_lane(0):_sum_lane(0) + 1]
        l_odd = odd[:, _sum_lane(1):_sum_lane(1) + 1]
        o_ref[0, :, p * LANES:(p + 1) * LANES] = jnp.where(lane < V_DIM, even / l_even, odd / l_odd).astype(
            o_ref.dtype)


def _attn_call(q, ktc, vc, kt, v):
    b, heads, s, _ = q.shape
    tq = _tile(s, 512)
    kb = kt.shape[4]
    resident = lambda a: pl.BlockSpec((1,) + a.shape[1:], lambda bi, i: (bi,) + (0,) * (a.ndim - 1),
                                      pipeline_mode=pl.Buffered(1))
    return pl.pallas_call(
        _attn_kernel, grid=(b, s // tq),
        in_specs=[pl.BlockSpec((1, heads, tq, LANES), lambda bi, i: (bi, 0, i, 0)),
                  resident(ktc), resident(vc), resident(kt), resident(v)],
        out_specs=pl.BlockSpec((1, tq, heads * V_DIM), lambda bi, i: (bi, i, 0)),
        out_shape=jax.ShapeDtypeStruct((b, s, heads * V_DIM), BF16),
        scratch_shapes=[pltpu.VMEM((heads, tq, LANES), F32), pltpu.VMEM((heads, tq, LANES), F32),
                        pltpu.VMEM((tq, kb), F32), pltpu.VMEM((tq, kb), F32),
                        pltpu.VMEM((tq, ktc.shape[4]), F32), pltpu.VMEM((tq, ktc.shape[4]), F32)],
        compiler_params=_params(("parallel", "arbitrary")), name="attn",
    )(q, ktc, vc, kt, v)


def _out_proj_kernel(x_ref, a_ref, t_ref, g_ref, wa_ref, wt_ref, o_ref):
    y = jnp.dot(a_ref[0], wa_ref[...], preferred_element_type=F32)
    y = y + jnp.dot(t_ref[0], wt_ref[...], preferred_element_type=F32)
    o_ref[0] = x_ref[0] + g_ref[0] * y


def _out_proj_call(x, a, att, g, wa, wt):
    b, s, d = x.shape
    ts = _tile(s, 1024)
    row = lambda c: pl.BlockSpec((1, ts, c), lambda bi, i: (bi, i, 0))
    full = lambda arr: pl.BlockSpec(arr.shape, lambda bi, i: (0,) * arr.ndim)
    return pl.pallas_call(
        _out_proj_kernel, grid=(b, s // ts),
        in_specs=[row(d), row(a.shape[2]), row(att.shape[2]),
                  pl.BlockSpec((1, 1, d), lambda bi, i: (bi, 0, 0)), full(wa), full(wt)],
        out_specs=row(d), out_shape=jax.ShapeDtypeStruct((b, s, d), F32),
        compiler_params=_params(("parallel", "arbitrary")), name="out_proj",
    )(x, a, att, g, wa, wt)


def _halo_hidden(xm_ref, xp_ref, xn_ref, sh_ref, sc_ref, ng_ref):
    i, n = pl.program_id(1), pl.num_programs(1)
    ts, d = xm_ref.shape[1], xm_ref.shape[2]
    xe = jnp.concatenate([xp_ref[0], xm_ref[0], xn_ref[0]], axis=0)
    h = _rms(xe, ng_ref[...], d) * (1.0 + sc_ref[0]) + sh_ref[0]
    row = lax.broadcasted_iota(jnp.int32, (ts + 2 * HALO, 1), 0)
    lo = jnp.where(i > 0, 0, HALO)
    hi = jnp.where(i < n - 1, ts + 2 * HALO, ts + HALO)
    valid = jnp.logical_and(row >= lo, row < hi)
    return jnp.where(valid, h, 0.0).astype(BF16)


def _conv3(z, w, ts):
    rows = z.shape[0]
    y = w[0:1] * pltpu.roll(z, 1, 0) + w[1:2] * z + w[2:3] * pltpu.roll(z, rows - 1, 0)
    return y[HALO:HALO + ts]


def _halo_specs(ts, s, d):
    hb, nh = ts // HALO, s // HALO
    return [pl.BlockSpec((1, ts, d), lambda bi, i: (bi, i, 0)),
            pl.BlockSpec((1, HALO, d), lambda bi, i: (bi, jnp.maximum(i * hb - 1, 0), 0)),
            pl.BlockSpec((1, HALO, d), lambda bi, i: (bi, jnp.minimum((i + 1) * hb, nh - 1), 0))]


def _gated_conv_kernel(xm_ref, xp_ref, xn_ref, sh_ref, sc_ref, g_ref, ng_ref, win_ref, cw_ref, cb_ref, wout_ref,
                       o_ref, acc_ref, h_ref, t0_ref, t1_ref, *, is_ffn):
    ts = xm_ref.shape[1]
    c = wout_ref.shape[1]
    t_refs = (t0_ref, t1_ref)
    h_ref[...] = _halo_hidden(xm_ref, xp_ref, xn_ref, sh_ref, sc_ref, ng_ref)
    acc_ref[...] = jnp.zeros(acc_ref.shape, F32)

    def project(k, slot):
        t_refs[slot][...] = jnp.dot(h_ref[...], win_ref[k], preferred_element_type=F32)

    def mix(k, slot):
        t_ref = t_refs[slot]
        tile_rows = pl.ds(HALO, ts)
        if is_ffn:
            conv = _conv3(t_ref[:, 0:c], cw_ref[k], ts) + cb_ref[k]
            y = _silu(conv) * t_ref[tile_rows, c:2 * c]
        else:
            conv = _conv3(t_ref[:, c:2 * c] * t_ref[:, 2 * c:3 * c], cw_ref[k], ts) + cb_ref[k]
            y = t_ref[tile_rows, 0:c] * conv
        acc_ref[...] += jnp.dot(y.astype(BF16), wout_ref[k], preferred_element_type=F32)

    _software_pipeline(win_ref.shape[0], project, mix, unroll=True)
    o_ref[0] = xm_ref[0] + g_ref[0] * acc_ref[...]


def _gated_conv_call(x, sh, sc, g, ng, win, cw, cb, wout, *, is_ffn, name):
    b, s, d = x.shape
    ts = _tile(s, 512)
    full = lambda a: pl.BlockSpec(a.shape, lambda bi, i: (0,) * a.ndim)
    vec = pl.BlockSpec((1, 1, d), lambda bi, i: (bi, 0, 0))
    t_buf = pltpu.VMEM((ts + 2 * HALO, win.shape[2]), F32)
    return pl.pallas_call(
        functools.partial(_gated_conv_kernel, is_ffn=is_ffn), grid=(b, s // ts),
        in_specs=_halo_specs(ts, s, d) + [vec, vec, vec, full(ng), full(win), full(cw), full(cb), full(wout)],
        out_specs=pl.BlockSpec((1, ts, d), lambda bi, i: (bi, i, 0)),
        out_shape=jax.ShapeDtypeStruct((b, s, d), F32),
        scratch_shapes=[pltpu.VMEM((ts, d), F32), pltpu.VMEM((ts + 2 * HALO, d), BF16), t_buf, t_buf],
        compiler_params=_params(("parallel", "arbitrary")), name=name,
    )(x, x, x, sh, sc, g, ng, win, cw, cb, wout)


def _rope_partner():
    quarter = QK_ROPE // 4
    partner = list(range(LANES))
    for g in range(QK_NOPE, QK_DIM, 2 * quarter):
        for l in range(g, g + quarter):
            partner[l], partner[l + quarter] = l + quarter, l
    return jnp.array(partner, jnp.int32)


def _rope_tables(n_tokens):
    half = QK_ROPE // 2
    quarter = half // 2
    inv = ROPE_THETA ** (-jnp.arange(0, half, 2, dtype=F32) / half)
    lane = jnp.arange(LANES)
    on = jnp.logical_and(lane >= QK_NOPE, lane < QK_DIM)[None, :]
    r = jnp.clip(lane - QK_NOPE, 0, QK_ROPE - 1)
    use_col = (r >= half)[None, :]
    second = ((r % half) >= quarter)[None, :]
    pos = jnp.arange(n_tokens)
    p = jnp.where(use_col, (pos % GRID_W)[:, None], (pos // GRID_W)[:, None]).astype(F32)
    ang = p * inv[r % quarter][None, :]
    sin = jnp.sin(ang)
    return jnp.where(on, jnp.cos(ang), 1.0), jnp.where(on, jnp.where(second, sin, -sin), 0.0)


def _pad_heads(w, heads, width):
    k = w.shape[0]
    w = w.reshape(k, heads, width)
    return jnp.pad(w, ((0, 0), (0, 0), (0, LANES - width))).reshape(k, heads * LANES)


def _chunk_cols(w, parts, chunk):
    d = w.shape[0]
    n = w.shape[1] // parts
    w = w.reshape(d, parts, n // chunk, chunk)
    return jnp.transpose(w, (2, 0, 1, 3)).reshape(n // chunk, d, parts * chunk)


def _chunk_vec(v, chunk):
    k, n = v.shape
    return jnp.transpose(v.reshape(k, n // chunk, chunk), (1, 0, 2))


def kernel(x, c, ctx, c_ctx, ada_w, ada_b, norm_mix_g, norm_ffn_g, ffn_w_up, ffn_conv_w, ffn_conv_b, ffn_w_down,
           ev_w_in, ev_conv_w, ev_conv_b, ev_ln_g, ev_ln_b, ev_qa_norm_g, ev_w_uq, ev_kva_norm_g, ev_w_ukv,
           ev_q_norm_g, ev_k_norm_g, ev_w_out, od_w_in, od_conv_w, od_conv_b, od_w_out):
    b, s, d = x.shape
    n_ctx = ctx.shape[1]
    depth = ada_w.shape[0]
    assert depth == 2, "kernel implements the two-layer (even, odd) block"
    conv_ch = ev_conv_w.shape[2]
    q_lora = ev_qa_norm_g.shape[1]
    kv_lora = ev_kva_norm_g.shape[1]
    heads = ev_w_uq.shape[2] // QK_DIM
    d_mix = od_conv_w.shape[2]
    ffn_dim = ffn_conv_w.shape[2]
    assert s % GRID_W == 0 and s % LANES == 0 and n_ctx % LANES == 0 and heads % 2 == 0
    assert ev_w_in.shape[2] == 2 * conv_ch + q_lora + kv_lora + QK_ROPE

    rows = -(-(b + 1) // HALO) * HALO
    cvec = jnp.concatenate([c, c_ctx[None, :], jnp.zeros((rows - b - 1, d), F32)], axis=0)
    mod = _mod_call(cvec, ada_w, ada_b)
    lat = lambda l, k: mod[l, :b, None, k * d:(k + 1) * d]
    ctxm = lambda l, k: jnp.broadcast_to(mod[l, b:b + 1, None, k * d:(k + 1) * d], (b, 1, d))

    w_in = ev_w_in[0]
    o_kr = 2 * conv_ch + q_lora + kv_lora
    partner = _rope_partner()
    kr_tile = jnp.pad(w_in[:, o_kr:], ((0, 0), (QK_NOPE, LANES - QK_DIM)))
    w1 = jnp.concatenate([w_in[:, :o_kr], kr_tile, kr_tile[:, partner]], axis=1).astype(BF16)
    wq = _pad_heads(ev_w_uq[0], heads, QK_DIM)
    wq_partner = wq.reshape(q_lora, heads, LANES)[:, :, partner].reshape(q_lora, heads * LANES)
    wq = jnp.concatenate([wq, wq_partner], axis=1).astype(BF16)
    wkv = ev_w_ukv[0].reshape(kv_lora, heads, QK_NOPE + V_DIM)
    wk = jnp.pad(wkv[:, :, :QK_NOPE], ((0, 0), (0, 0), (0, LANES - QK_NOPE))).reshape(kv_lora, heads * LANES)
    wk = wk.astype(BF16)
    wv3 = wkv[:, :, QK_NOPE:]
    zv = jnp.zeros_like(wv3)
    even_head = (jnp.arange(heads) % 2 == 0)[None, :, None]
    wv = jnp.where(even_head, jnp.concatenate([wv3, zv], -1), jnp.concatenate([zv, wv3], -1))
    wv = wv.reshape(kv_lora, heads * LANES).astype(BF16)
    gq = jnp.pad(ev_q_norm_g[0], (0, LANES - QK_DIM))[None, :]
    gk = jnp.pad(ev_k_norm_g[0], (0, LANES - QK_DIM))[None, :]
    even_kw = dict(heads=heads, conv_ch=conv_ch, q_lora=q_lora, kv_lora=kv_lora)
    even_w = (norm_mix_g[0:1], w1, ev_qa_norm_g, wq, ev_kva_norm_g, wk, wv, gq, gq[:, partner], gk, gk[:, partner])

    cos, ssin = _rope_tables(s)
    ident = (jnp.ones((n_ctx, LANES), F32), jnp.zeros((n_ctx, LANES), F32))

    ktc, vc = _even_in_call(ctx, ctxm(0, 0), ctxm(0, 1), *even_w, *ident, with_q=False, name="even_in_ctx",
                            **even_kw)
    u, q, kt, v = _even_in_call(x, lat(0, 0), lat(0, 1), *even_w, cos, ssin, with_q=True, name="even_in",
                                **even_kw)
    a = _conf_conv_call(u, ev_conv_w[0], ev_conv_b, ev_ln_g, ev_ln_b)
    att = _attn_call(q, ktc, vc, kt, v)
    w_out = ev_w_out[0].astype(BF16)
    x = _out_proj_call(x, a, att, lat(0, 2), w_out[:conv_ch], w_out[conv_ch:])

    fc = MXU_WIDTH if ffn_dim % MXU_WIDTH == 0 else LANES

    def ffn(xin, l):
        wup = _chunk_cols(ffn_w_up[l], 2, fc).astype(BF16)
        wdn = ffn_w_down[l].reshape(ffn_dim // fc, fc, d).astype(BF16)
        return _gated_conv_call(xin, lat(l, 3), lat(l, 4), lat(l, 5), norm_ffn_g[l:l + 1], wup,
                                _chunk_vec(ffn_conv_w[l], fc), _chunk_vec(ffn_conv_b[l:l + 1], fc), wdn,
                                is_ffn=True, name="ffn")

    x = ffn(x, 0)

    mc = MXU_WIDTH if d_mix % MXU_WIDTH == 0 else LANES
    win = _chunk_cols(od_w_in[0], 3, mc).astype(BF16)
    wout = od_w_out[0].reshape(d_mix // mc, mc, d).astype(BF16)
    x = _gated_conv_call(x, lat(1, 0), lat(1, 1), lat(1, 2), norm_mix_g[1:2], win, _chunk_vec(od_conv_w[0], mc),
                         _chunk_vec(od_conv_b, mc), wout, is_ffn=False, name="odd_mix")
    return ffn(x, 1)
```

```python
import functools
import math

import jax
import jax.numpy as jnp
from jax import lax
from jax.experimental import pallas as pl
from jax.experimental.pallas import tpu as pltpu

F32 = jnp.float32
BF16 = jnp.bfloat16

GRID_W = 64
QK_NOPE = 64
QK_ROPE = 32
V_DIM = 64
QK_DIM = QK_NOPE + QK_ROPE
ROPE_THETA = 10000.0
EPS = 1e-6
SM_SCALE = QK_DIM ** -0.5
Q_SCALE = SM_SCALE * math.log2(math.e)
N_MOD = 6

LANES = 128
SUBLANES = 8
MXU_WIDTH = 256
VMEM_LIMIT = 60000 * 1024

ATTN_KB = 1024
ATTN_GROUP = 4
CONF_ROWS = 32
HALO = SUBLANES
CONF_HALO = 16
NEG_BIG = -1e30


def _tile(n, pref):
    t = min(n, pref)
    while n % t:
        t //= 2
    return t


def _rms(x, g, n):
    ms = jnp.sum(x * x, axis=-1, keepdims=True) * (1.0 / n)
    return x * lax.rsqrt(ms + EPS) * g


def _silu(x):
    return x * jax.nn.sigmoid(x)


def _params(sem):
    return pltpu.CompilerParams(dimension_semantics=sem, vmem_limit_bytes=VMEM_LIMIT)


def _software_pipeline(n, produce, consume, *, group=2, unroll=False, filler=None):
    assert group % 2 == 0
    produce(0, 0)
    n_loop = 0 if unroll else (n - 1) // group

    def body(u, carry):
        for j in range(group):
            produce(group * u + j + 1, (j + 1) % 2)
            consume(group * u + j, j % 2)
        if filler is not None:
            filler(u)
        return carry
    lax.fori_loop(0, n_loop, body, 0)
    for t in range(group * n_loop, n):
        if t + 1 < n:
            produce(t + 1, (t + 1) % 2)
        consume(t, t % 2)
    if filler is not None:
        filler(n_loop)


def _pipeline_groups(n, group):
    return (n - 1) // group + 1


def _mod_kernel(c_ref, w_ref, b_ref, o_ref):
    s = _silu(c_ref[...])
    o_ref[0] = jnp.dot(s, w_ref[0], preferred_element_type=F32,
                       precision=lax.Precision.HIGHEST) + b_ref[0]


def _mod_call(cvec, ada_w, ada_b):
    depth, d, nd = ada_w.shape
    rows = cvec.shape[0]
    tn = _tile(nd, 1536)
    return pl.pallas_call(
        _mod_kernel,
        grid=(depth, nd // tn),
        in_specs=[
            pl.BlockSpec((rows, d), lambda l, j: (0, 0)),
            pl.BlockSpec((1, d, tn), lambda l, j: (l, 0, j)),
            pl.BlockSpec((1, 1, tn), lambda l, j: (l, 0, j)),
        ],
        out_specs=pl.BlockSpec((1, rows, tn), lambda l, j: (l, 0, j)),
        out_shape=jax.ShapeDtypeStruct((depth, rows, nd), F32),
        compiler_params=_params(("arbitrary", "arbitrary")),
        name="mod",
    )(cvec, ada_w, ada_b.reshape(depth, 1, nd))


def _even_in_kernel(x_ref, sh_ref, sc_ref, gmix_ref, w1_ref, qag_ref, wq_ref, kvag_ref, wk_ref, wv_ref,
                    gq_ref, gqp_ref, gk_ref, gkp_ref, cos_ref, ssin_ref, *out_refs, heads, conv_ch, q_lora,
                    kv_lora, with_q):
    if with_q:
        u_ref, q_ref, kt_ref, v_ref = out_refs
    else:
        kt_ref, v_ref = out_refs
    ts, d = x_ref.shape[1], x_ref.shape[2]
    x = x_ref[0]
    h = _rms(x, gmix_ref[...], d) * (1.0 + sc_ref[0]) + sh_ref[0]
    proj = jnp.dot(h.astype(BF16), w1_ref[...], preferred_element_type=F32)
    o_cq = 2 * conv_ch
    o_ckv = o_cq + q_lora
    o_kr = o_ckv + kv_lora
    cos, ssin = cos_ref[...], ssin_ref[...]
    lane = lax.broadcasted_iota(jnp.int32, (1, LANES), 1)

    if with_q:
        u_ref[0] = proj[:, :conv_ch] * jax.nn.sigmoid(proj[:, conv_ch:o_cq])
        cqn = _rms(proj[:, o_cq:o_ckv], qag_ref[...], q_lora)
        qf = jnp.dot(cqn.astype(BF16), wq_ref[...], preferred_element_type=F32)
        q_cos = gq_ref[...] * cos
        q_sin = gqp_ref[...] * ssin
        for hd in range(heads):
            qh = qf[:, hd * LANES:(hd + 1) * LANES]
            qp = qf[:, (heads + hd) * LANES:(heads + hd + 1) * LANES]
            r = lax.rsqrt(jnp.sum(qh * qh, axis=-1, keepdims=True) * (1.0 / QK_DIM) + EPS)
            q_ref[0, hd] = ((qh * q_cos + qp * q_sin) * (r * Q_SCALE)).astype(BF16)

    ckvn = _rms(proj[:, o_ckv:o_kr], kvag_ref[...], kv_lora).astype(BF16)
    kf = jnp.dot(ckvn, wk_ref[...], preferred_element_type=F32)
    vf = jnp.dot(ckvn, wv_ref[...], preferred_element_type=F32)
    kr = proj[:, o_kr:o_kr + LANES]
    krp = proj[:, o_kr + LANES:o_kr + 2 * LANES]
    gk = gk_ref[...]
    k_rot = kr * (gk * cos) + krp * (gkp_ref[...] * ssin)
    ss_rot = jnp.sum(kr * kr, axis=-1, keepdims=True)
    for hd in range(heads):
        kn = kf[:, hd * LANES:(hd + 1) * LANES]
        r = lax.rsqrt((jnp.sum(kn * kn, axis=-1, keepdims=True) + ss_rot) * (1.0 / QK_DIM) + EPS)
        kt_ref[0, hd, 0] = ((kn * gk + k_rot) * r).T.astype(BF16)
        one_hot = jnp.where(lane == _sum_lane(hd), 1.0, 0.0)
        v_ref[0, hd] = (vf[:, hd * LANES:(hd + 1) * LANES] + one_hot).astype(BF16)


def _sum_lane(head):
    return V_DIM if head % 2 == 0 else 0


def _even_in_call(x, sh, sc, gmix, w1, qag, wq, kvag, wk, wv, gq, gqp, gk, gkp, cos, ssin, *, heads, conv_ch,
                  q_lora, kv_lora, with_q, name):
    b, s, d = x.shape
    ts = _tile(s, 512)
    kb = _tile(s, ATTN_KB)
    r = kb // ts
    kern = functools.partial(_even_in_kernel, heads=heads, conv_ch=conv_ch, q_lora=q_lora, kv_lora=kv_lora,
                             with_q=with_q)
    full = lambda a: pl.BlockSpec(a.shape, lambda bi, i: (0,) * a.ndim)
    vec = pl.BlockSpec((1, 1, d), lambda bi, i: (bi, 0, 0))
    tab = pl.BlockSpec((ts, LANES), lambda bi, i: (i, 0))
    in_specs = [pl.BlockSpec((1, ts, d), lambda bi, i: (bi, i, 0)), vec, vec, full(gmix), full(w1), full(qag),
                full(wq), full(kvag), full(wk), full(wv), full(gq), full(gqp), full(gk), full(gkp), tab, tab]
    head_rows = pl.BlockSpec((1, heads, ts, LANES), lambda bi, i: (bi, 0, i, 0))
    out_specs = [pl.BlockSpec((1, heads, 1, LANES, ts), lambda bi, i: (bi, 0, i // r, 0, i % r)), head_rows]
    out_shape = [jax.ShapeDtypeStruct((b, heads, s // kb, LANES, kb), BF16),
                 jax.ShapeDtypeStruct((b, heads, s, LANES), BF16)]
    if with_q:
        out_specs = [pl.BlockSpec((1, ts, conv_ch), lambda bi, i: (bi, i, 0)), head_rows] + out_specs
        out_shape = [jax.ShapeDtypeStruct((b, s, conv_ch), F32),
                     jax.ShapeDtypeStruct((b, heads, s, LANES), BF16)] + out_shape
    return pl.pallas_call(
        kern, grid=(b, s // ts), in_specs=in_specs, out_specs=out_specs, out_shape=out_shape,
        compiler_params=_params(("parallel", "arbitrary")), name=name,
    )(x, sh, sc, gmix, w1, qag, wq, kvag, wk, wv, gq, gqp, gk, gkp, cos, ssin)


def _softmax_pv(h, s, v_blk, acc_ref, m_ref):
    nt = s.shape[1] // LANES
    tile = lambda t: s[:, t * LANES:(t + 1) * LANES]
    mx = tile(0)
    for t in range(1, nt):
        mx = jnp.maximum(mx, tile(t))
    m_old = m_ref[h]
    m_new = jnp.maximum(m_old, jnp.max(mx, axis=-1, keepdims=True))
    m_ref[h] = m_new
    pm = jnp.concatenate([jnp.exp2(tile(t) - m_new).astype(BF16) for t in range(nt)], axis=1)
    pv = jnp.dot(pm, v_blk, preferred_element_type=F32)
    acc_ref[h] = acc_ref[h] * jnp.exp2(m_old - m_new) + pv


def _conf_conv_rows(r0, buf_ref, win_ref, w_ref, b_ref, g_ref, beta_ref, a_ref):
    taps = w_ref.shape[0]
    first = CONF_HALO - (taps - 1) // 2
    r0 = pl.multiple_of(r0, CONF_ROWS)
    span = CONF_ROWS + (first + taps - 1) // SUBLANES * SUBLANES
    window = buf_ref[pl.ds(r0, span + SUBLANES), :]
    for j in range(1, SUBLANES):
        win_ref[j - 1] = window[j:j + span]
    acc = jnp.zeros((CONF_ROWS, buf_ref.shape[1]), F32)
    for k in range(taps):
        q, j = divmod(first + k, SUBLANES)
        if j == 0:
            tap = buf_ref[pl.ds(r0 + q * SUBLANES, CONF_ROWS), :]
        else:
            tap = win_ref[j - 1, q * SUBLANES:q * SUBLANES + CONF_ROWS, :]
        acc = acc + w_ref[k:k + 1, :] * tap
    u = acc + b_ref[...]
    mu = jnp.mean(u, axis=-1, keepdims=True)
    xc = u - mu
    y = xc * lax.rsqrt(jnp.mean(xc * xc, axis=-1, keepdims=True) + EPS) * g_ref[...] + beta_ref[...]
    a_ref[0, pl.ds(r0, CONF_ROWS), :] = _silu(y).astype(a_ref.dtype)


def _attn_kernel(q_ref, ktc_ref, vc_ref, kt_ref, v_ref, um_ref, up_ref, un_ref, cw_ref, cb_ref, lg_ref, lb_ref,
                 o_ref, a_ref, acc_ref, m_ref, s0_ref, s1_ref, c0_ref, c1_ref, buf_ref, win_ref):
    heads, nkb, kb = kt_ref.shape[1], kt_ref.shape[2], kt_ref.shape[4]
    tq = q_ref.shape[2]
    n_items = heads * nkb
    s_refs = (s0_ref, s1_ref)
    c_refs = (c0_ref, c1_ref)
    i, n = pl.program_id(1), pl.num_programs(1)
    acc_ref[...] = jnp.zeros(acc_ref.shape, F32)
    m_ref[...] = jnp.full(m_ref.shape, NEG_BIG, F32)
    buf_ref[0:CONF_HALO] = jnp.where(i > 0, up_ref[0], 0.0)
    buf_ref[CONF_HALO:CONF_HALO + tq] = um_ref[0]
    buf_ref[CONF_HALO + tq:] = jnp.where(i < n - 1, un_ref[0], 0.0)

    def ctx_scores(h, slot):
        c_refs[slot][...] = jnp.dot(q_ref[0, h], ktc_ref[0, h, 0], preferred_element_type=F32)

    def ctx_fold(h, slot):
        _softmax_pv(h, c_refs[slot][...], vc_ref[0, h], acc_ref, m_ref)

    _software_pipeline(heads, ctx_scores, ctx_fold)

    def scores(t, slot):
        s_refs[slot][...] = jnp.dot(q_ref[0, t // nkb], kt_ref[0, t // nkb, t % nkb],
                                    preferred_element_type=F32)

    def fold(t, slot):
        rows = pl.ds(pl.multiple_of((t % nkb) * kb, kb), kb)
        _softmax_pv(t // nkb, s_refs[slot][...], v_ref[0, t // nkb, rows, :], acc_ref, m_ref)

    rows_per_body = tq // _pipeline_groups(n_items, ATTN_GROUP)

    def conf_rows(g):
        for c in range(rows_per_body // CONF_ROWS):
            _conf_conv_rows(g * rows_per_body + c * CONF_ROWS, buf_ref, win_ref, cw_ref, cb_ref, lg_ref, lb_ref,
                            a_ref)

    _software_pipeline(n_items, scores, fold, group=ATTN_GROUP, filler=conf_rows)

    lane = lax.broadcasted_iota(jnp.int32, (1, LANES), 1)
    for p in range(heads // 2):
        even, odd = acc_ref[2 * p], acc_ref[2 * p + 1]
        l_even = even[:, _sum_lane(0):_sum_lane(0) + 1]
        l_odd = odd[:, _sum_lane(1):_sum_lane(1) + 1]
        o_ref[0, :, p * LANES:(p + 1) * LANES] = jnp.where(lane < V_DIM, even / l_even, odd / l_odd).astype(
            o_ref.dtype)


def _attn_call(q, ktc, vc, kt, v, u, cw, cb, lg, lb):
    b, heads, s, _ = q.shape
    ch = u.shape[2]
    tq = _tile(s, 512)
    kb = kt.shape[4]
    taps = cw.shape[0]
    groups = _pipeline_groups(heads * kt.shape[2], ATTN_GROUP)
    assert (taps - 1) // 2 <= CONF_HALO and tq % (groups * CONF_ROWS) == 0
    hb, nh = tq // CONF_HALO, s // CONF_HALO
    span = CONF_ROWS + (CONF_HALO - (taps - 1) // 2 + taps - 1) // SUBLANES * SUBLANES
    resident = lambda a: pl.BlockSpec((1,) + a.shape[1:], lambda bi, i: (bi,) + (0,) * (a.ndim - 1),
                                      pipeline_mode=pl.Buffered(1))
    full = lambda a: pl.BlockSpec(a.shape, lambda bi, i: (0,) * a.ndim)
    rows = lambda c: pl.BlockSpec((1, tq, c), lambda bi, i: (bi, i, 0))
    return pl.pallas_call(
        _attn_kernel, grid=(b, s // tq),
        in_specs=[pl.BlockSpec((1, heads, tq, LANES), lambda bi, i: (bi, 0, i, 0)),
                  resident(ktc), resident(vc), resident(kt), resident(v),
                  rows(ch),
                  pl.BlockSpec((1, CONF_HALO, ch), lambda bi, i: (bi, jnp.maximum(i * hb - 1, 0), 0)),
                  pl.BlockSpec((1, CONF_HALO, ch), lambda bi, i: (bi, jnp.minimum((i + 1) * hb, nh - 1), 0)),
                  full(cw), full(cb), full(lg), full(lb)],
        out_specs=[rows(heads * V_DIM), rows(ch)],
        out_shape=[jax.ShapeDtypeStruct((b, s, heads * V_DIM), BF16), jax.ShapeDtypeStruct((b, s, ch), BF16)],
        scratch_shapes=[pltpu.VMEM((heads, tq, LANES), F32), pltpu.VMEM((heads, tq, LANES), F32),
                        pltpu.VMEM((tq, kb), F32), pltpu.VMEM((tq, kb), F32),
                        pltpu.VMEM((tq, ktc.shape[4]), F32), pltpu.VMEM((tq, ktc.shape[4]), F32),
                        pltpu.VMEM((tq + 2 * CONF_HALO, ch), F32), pltpu.VMEM((SUBLANES - 1, span, ch), F32)],
        compiler_params=_params(("parallel", "arbitrary")), name="attn",
    )(q, ktc, vc, kt, v, u, u, u, cw, cb, lg, lb)


def _out_proj_kernel(x_ref, a_ref, t_ref, g_ref, wa_ref, wt_ref, o_ref):
    y = jnp.dot(a_ref[0], wa_ref[...], preferred_element_type=F32)
    y = y + jnp.dot(t_ref[0], wt_ref[...], preferred_element_type=F32)
    o_ref[0] = x_ref[0] + g_ref[0] * y


def _out_proj_call(x, a, att, g, wa, wt):
    b, s, d = x.shape
    ts = _tile(s, 1024)
    row = lambda c: pl.BlockSpec((1, ts, c), lambda bi, i: (bi, i, 0))
    full = lambda arr: pl.BlockSpec(arr.shape, lambda bi, i: (0,) * arr.ndim)
    return pl.pallas_call(
        _out_proj_kernel, grid=(b, s // ts),
        in_specs=[row(d), row(a.shape[2]), row(att.shape[2]),
                  pl.BlockSpec((1, 1, d), lambda bi, i: (bi, 0, 0)), full(wa), full(wt)],
        out_specs=row(d), out_shape=jax.ShapeDtypeStruct((b, s, d), F32),
        compiler_params=_params(("parallel", "arbitrary")), name="out_proj",
    )(x, a, att, g, wa, wt)


def _halo_hidden(xm_ref, xp_ref, xn_ref, sh_ref, sc_ref, ng_ref):
    i, n = pl.program_id(1), pl.num_programs(1)
    ts, d = xm_ref.shape[1], xm_ref.shape[2]
    xe = jnp.concatenate([xp_ref[0], xm_ref[0], xn_ref[0]], axis=0)
    h = _rms(xe, ng_ref[...], d) * (1.0 + sc_ref[0]) + sh_ref[0]
    row = lax.broadcasted_iota(jnp.int32, (ts + 2 * HALO, 1), 0)
    lo = jnp.where(i > 0, 0, HALO)
    hi = jnp.where(i < n - 1, ts + 2 * HALO, ts + HALO)
    valid = jnp.logical_and(row >= lo, row < hi)
    return jnp.where(valid, h, 0.0).astype(BF16)


def _conv3(z, w, ts):
    rows = z.shape[0]
    y = w[0:1] * pltpu.roll(z, 1, 0) + w[1:2] * z + w[2:3] * pltpu.roll(z, rows - 1, 0)
    return y[HALO:HALO + ts]


def _halo_specs(ts, s, d):
    hb, nh = ts // HALO, s // HALO
    return [pl.BlockSpec((1, ts, d), lambda bi, i: (bi, i, 0)),
            pl.BlockSpec((1, HALO, d), lambda bi, i: (bi, jnp.maximum(i * hb - 1, 0), 0)),
            pl.BlockSpec((1, HALO, d), lambda bi, i: (bi, jnp.minimum((i + 1) * hb, nh - 1), 0))]


def _gated_conv_kernel(xm_ref, xp_ref, xn_ref, sh_ref, sc_ref, g_ref, ng_ref, win_ref, cw_ref, cb_ref, wout_ref,
                       o_ref, acc_ref, h_ref, t0_ref, t1_ref, *, is_ffn):
    ts = xm_ref.shape[1]
    c = wout_ref.shape[1]
    t_refs = (t0_ref, t1_ref)
    h_ref[...] = _halo_hidden(xm_ref, xp_ref, xn_ref, sh_ref, sc_ref, ng_ref)
    acc_ref[...] = jnp.zeros(acc_ref.shape, F32)

    def project(k, slot):
        t_refs[slot][...] = jnp.dot(h_ref[...], win_ref[k], preferred_element_type=F32)

    def mix(k, slot):
        t_ref = t_refs[slot]
        tile_rows = pl.ds(HALO, ts)
        if is_ffn:
            conv = _conv3(t_ref[:, 0:c], cw_ref[k], ts) + cb_ref[k]
            y = _silu(conv) * t_ref[tile_rows, c:2 * c]
        else:
            conv = _conv3(t_ref[:, c:2 * c] * t_ref[:, 2 * c:3 * c], cw_ref[k], ts) + cb_ref[k]
            y = t_ref[tile_rows, 0:c] * conv
        acc_ref[...] += jnp.dot(y.astype(BF16), wout_ref[k], preferred_element_type=F32)

    _software_pipeline(win_ref.shape[0], project, mix, unroll=True)
    o_ref[0] = xm_ref[0] + g_ref[0] * acc_ref[...]


def _gated_conv_call(x, sh, sc, g, ng, win, cw, cb, wout, *, is_ffn, name):
    b, s, d = x.shape
    ts = _tile(s, 512)
    full = lambda a: pl.BlockSpec(a.shape, lambda bi, i: (0,) * a.ndim)
    vec = pl.BlockSpec((1, 1, d), lambda bi, i: (bi, 0, 0))
    t_buf = pltpu.VMEM((ts + 2 * HALO, win.shape[2]), F32)
    return pl.pallas_call(
        functools.partial(_gated_conv_kernel, is_ffn=is_ffn), grid=(b, s // ts),
        in_specs=_halo_specs(ts, s, d) + [vec, vec, vec, full(ng), full(win), full(cw), full(cb), full(wout)],
        out_specs=pl.BlockSpec((1, ts, d), lambda bi, i: (bi, i, 0)),
        out_shape=jax.ShapeDtypeStruct((b, s, d), F32),
        scratch_shapes=[pltpu.VMEM((ts, d), F32), pltpu.VMEM((ts + 2 * HALO, d), BF16), t_buf, t_buf],
        compiler_params=_params(("parallel", "arbitrary")), name=name,
    )(x, x, x, sh, sc, g, ng, win, cw, cb, wout)


def _rope_partner():
    quarter = QK_ROPE // 4
    partner = list(range(LANES))
    for g in range(QK_NOPE, QK_DIM, 2 * quarter):
        for l in range(g, g + quarter):
            partner[l], partner[l + quarter] = l + quarter, l
    return jnp.array(partner, jnp.int32)


def _rope_tables(n_tokens):
    half = QK_ROPE // 2
    quarter = half // 2
    inv = ROPE_THETA ** (-jnp.arange(0, half, 2, dtype=F32) / half)
    lane = jnp.arange(LANES)
    on = jnp.logical_and(lane >= QK_NOPE, lane < QK_DIM)[None, :]
    r = jnp.clip(lane - QK_NOPE, 0, QK_ROPE - 1)
    use_col = (r >= half)[None, :]
    second = ((r % half) >= quarter)[None, :]
    pos = jnp.arange(n_tokens)
    p = jnp.where(use_col, (pos % GRID_W)[:, None], (pos // GRID_W)[:, None]).astype(F32)
    ang = p * inv[r % quarter][None, :]
    sin = jnp.sin(ang)
    return jnp.where(on, jnp.cos(ang), 1.0), jnp.where(on, jnp.where(second, sin, -sin), 0.0)


def _pad_heads(w, heads, width):
    k = w.shape[0]
    w = w.reshape(k, heads, width)
    return jnp.pad(w, ((0, 0), (0, 0), (0, LANES - width))).reshape(k, heads * LANES)


def _chunk_cols(w, parts, chunk):
    d = w.shape[0]
    n = w.shape[1] // parts
    w = w.reshape(d, parts, n // chunk, chunk)
    return jnp.transpose(w, (2, 0, 1, 3)).reshape(n // chunk, d, parts * chunk)


def _chunk_vec(v, chunk):
    k, n = v.shape
    return jnp.transpose(v.reshape(k, n // chunk, chunk), (1, 0, 2))


def kernel(x, c, ctx, c_ctx, ada_w, ada_b, norm_mix_g, norm_ffn_g, ffn_w_up, ffn_conv_w, ffn_conv_b, ffn_w_down,
           ev_w_in, ev_conv_w, ev_conv_b, ev_ln_g, ev_ln_b, ev_qa_norm_g, ev_w_uq, ev_kva_norm_g, ev_w_ukv,
           ev_q_norm_g, ev_k_norm_g, ev_w_out, od_w_in, od_conv_w, od_conv_b, od_w_out):
    b, s, d = x.shape
    n_ctx = ctx.shape[1]
    depth = ada_w.shape[0]
    assert depth == 2, "kernel implements the two-layer (even, odd) block"
    conv_ch = ev_conv_w.shape[2]
    q_lora = ev_qa_norm_g.shape[1]
    kv_lora = ev_kva_norm_g.shape[1]
    heads = ev_w_uq.shape[2] // QK_DIM
    d_mix = od_conv_w.shape[2]
    ffn_dim = ffn_conv_w.shape[2]
    assert s % GRID_W == 0 and s % LANES == 0 and n_ctx % LANES == 0 and heads % 2 == 0
    assert ev_w_in.shape[2] == 2 * conv_ch + q_lora + kv_lora + QK_ROPE

    rows = -(-(b + 1) // HALO) * HALO
    cvec = jnp.concatenate([c, c_ctx[None, :], jnp.zeros((rows - b - 1, d), F32)], axis=0)
    mod = _mod_call(cvec, ada_w, ada_b)
    lat = lambda l, k: mod[l, :b, None, k * d:(k + 1) * d]
    ctxm = lambda l, k: jnp.broadcast_to(mod[l, b:b + 1, None, k * d:(k + 1) * d], (b, 1, d))

    w_in = ev_w_in[0]
    o_kr = 2 * conv_ch + q_lora + kv_lora
    partner = _rope_partner()
    kr_tile = jnp.pad(w_in[:, o_kr:], ((0, 0), (QK_NOPE, LANES - QK_DIM)))
    w1 = jnp.concatenate([w_in[:, :o_kr], kr_tile, kr_tile[:, partner]], axis=1).astype(BF16)
    wq = _pad_heads(ev_w_uq[0], heads, QK_DIM)
    wq_partner = wq.reshape(q_lora, heads, LANES)[:, :, partner].reshape(q_lora, heads * LANES)
    wq = jnp.concatenate([wq, wq_partner], axis=1).astype(BF16)
    wkv = ev_w_ukv[0].reshape(kv_lora, heads, QK_NOPE + V_DIM)
    wk = jnp.pad(wkv[:, :, :QK_NOPE], ((0, 0), (0, 0), (0, LANES - QK_NOPE))).reshape(kv_lora, heads * LANES)
    wk = wk.astype(BF16)
    wv3 = wkv[:, :, QK_NOPE:]
    zv = jnp.zeros_like(wv3)
    even_head = (jnp.arange(heads) % 2 == 0)[None, :, None]
    wv = jnp.where(even_head, jnp.concatenate([wv3, zv], -1), jnp.concatenate([zv, wv3], -1))
    wv = wv.reshape(kv_lora, heads * LANES).astype(BF16)
    gq = jnp.pad(ev_q_norm_g[0], (0, LANES - QK_DIM))[None, :]
    gk = jnp.pad(ev_k_norm_g[0], (0, LANES - QK_DIM))[None, :]
    even_kw = dict(heads=heads, conv_ch=conv_ch, q_lora=q_lora, kv_lora=kv_lora)
    even_w = (norm_mix_g[0:1], w1, ev_qa_norm_g, wq, ev_kva_norm_g, wk, wv, gq, gq[:, partner], gk, gk[:, partner])

    cos, ssin = _rope_tables(s)
    ident = (jnp.ones((n_ctx, LANES), F32), jnp.zeros((n_ctx, LANES), F32))

    ktc, vc = _even_in_call(ctx, ctxm(0, 0), ctxm(0, 1), *even_w, *ident, with_q=False, name="even_in_ctx",
                            **even_kw)
    u, q, kt, v = _even_in_call(x, lat(0, 0), lat(0, 1), *even_w, cos, ssin, with_q=True, name="even_in",
                                **even_kw)
    att, a = _attn_call(q, ktc, vc, kt, v, u, ev_conv_w[0], ev_conv_b, ev_ln_g, ev_ln_b)
    w_out = ev_w_out[0].astype(BF16)
    x = _out_proj_call(x, a, att, lat(0, 2), w_out[:conv_ch], w_out[conv_ch:])

    fc = MXU_WIDTH if ffn_dim % MXU_WIDTH == 0 else LANES

    def ffn(xin, l):
        wup = _chunk_cols(ffn_w_up[l], 2, fc).astype(BF16)
        wdn = ffn_w_down[l].reshape(ffn_dim // fc, fc, d).astype(BF16)
        return _gated_conv_call(xin, lat(l, 3), lat(l, 4), lat(l, 5), norm_ffn_g[l:l + 1], wup,
                                _chunk_vec(ffn_conv_w[l], fc), _chunk_vec(ffn_conv_b[l:l + 1], fc), wdn,
                                is_ffn=True, name="ffn")

    x = ffn(x, 0)

    mc = MXU_WIDTH if d_mix % MXU_WIDTH == 0 else LANES
    win = _chunk_cols(od_w_in[0], 3, mc).astype(BF16)
    wout = od_w_out[0].reshape(d_mix // mc, mc, d).astype(BF16)
    x = _gated_conv_call(x, lat(1, 0), lat(1, 1), lat(1, 2), norm_mix_g[1:2], win, _chunk_vec(od_conv_w[0], mc),
                         _chunk_vec(od_conv_b, mc), wout, is_ffn=False, name="odd_mix")
    return ffn(x, 1)
```

```python
import functools
import math

import jax
import jax.numpy as jnp
from jax import lax
from jax.experimental import pallas as pl
from jax.experimental.pallas import tpu as pltpu

F32 = jnp.float32
BF16 = jnp.bfloat16

GRID_W = 64
QK_NOPE = 64
QK_ROPE = 32
V_DIM = 64
QK_DIM = QK_NOPE + QK_ROPE
BIAS_LANE = QK_DIM
ROPE_THETA = 10000.0
EPS = 1e-6
SM_SCALE = QK_DIM ** -0.5
Q_SCALE = SM_SCALE * math.log2(math.e)
N_MOD = 6

LANES = 128
SUBLANES = 8
MXU_WIDTH = 256
VMEM_LIMIT = 60000 * 1024

ATTN_KB = 1024
ATTN_GROUP = 4
CONF_ROWS = 32
HALO = SUBLANES
CONF_HALO = 16
NEG_BIG = -1e30
MAX_STATIC_SHIFT = 40.0


def _tile(n, pref):
    t = min(n, pref)
    while n % t:
        t //= 2
    return t


def _rms(x, g, n):
    ms = jnp.sum(x * x, axis=-1, keepdims=True) * (1.0 / n)
    return x * lax.rsqrt(ms + EPS) * g


def _silu(x):
    return x * jax.nn.sigmoid(x)


def _params(sem):
    return pltpu.CompilerParams(dimension_semantics=sem, vmem_limit_bytes=VMEM_LIMIT)


def _software_pipeline(n, produce, consume, *, group=2, unroll=False, filler=None):
    assert group % 2 == 0
    produce(0, 0)
    n_loop = 0 if unroll else (n - 1) // group

    def body(u, carry):
        for j in range(group):
            produce(group * u + j + 1, (j + 1) % 2)
            consume(group * u + j, j % 2)
        if filler is not None:
            filler(u)
        return carry
    lax.fori_loop(0, n_loop, body, 0)
    for t in range(group * n_loop, n):
        if t + 1 < n:
            produce(t + 1, (t + 1) % 2)
        consume(t, t % 2)
    if filler is not None:
        filler(n_loop)


def _pipeline_groups(n, group):
    return (n - 1) // group + 1


def _mod_kernel(c_ref, w_ref, b_ref, o_ref):
    s = _silu(c_ref[...])
    o_ref[0] = jnp.dot(s, w_ref[0], preferred_element_type=F32,
                       precision=lax.Precision.HIGHEST) + b_ref[0]


def _mod_call(cvec, ada_w, ada_b):
    depth, d, nd = ada_w.shape
    rows = cvec.shape[0]
    tn = _tile(nd, 1536)
    return pl.pallas_call(
        _mod_kernel,
        grid=(depth, nd // tn),
        in_specs=[
            pl.BlockSpec((rows, d), lambda l, j: (0, 0)),
            pl.BlockSpec((1, d, tn), lambda l, j: (l, 0, j)),
            pl.BlockSpec((1, 1, tn), lambda l, j: (l, 0, j)),
        ],
        out_specs=pl.BlockSpec((1, rows, tn), lambda l, j: (l, 0, j)),
        out_shape=jax.ShapeDtypeStruct((depth, rows, nd), F32),
        compiler_params=_params(("arbitrary", "arbitrary")),
        name="mod",
    )(cvec, ada_w, ada_b.reshape(depth, 1, nd))


def _even_in_kernel(x_ref, sh_ref, sc_ref, gmix_ref, w1_ref, qag_ref, wq_ref, kvag_ref, wk_ref, wv_ref,
                    gq_ref, gqp_ref, gk_ref, gkp_ref, kbias_ref, cos_ref, ssin_ref, *out_refs, heads, conv_ch, q_lora,
                    kv_lora, with_q):
    if with_q:
        u_ref, q_ref, kt_ref, v_ref = out_refs
    else:
        kt_ref, v_ref = out_refs
    ts, d = x_ref.shape[1], x_ref.shape[2]
    x = x_ref[0]
    h = _rms(x, gmix_ref[...], d) * (1.0 + sc_ref[0]) + sh_ref[0]
    proj = jnp.dot(h.astype(BF16), w1_ref[...], preferred_element_type=F32)
    o_cq = 2 * conv_ch
    o_ckv = o_cq + q_lora
    o_kr = o_ckv + kv_lora
    cos, ssin = cos_ref[...], ssin_ref[...]
    lane = lax.broadcasted_iota(jnp.int32, (1, LANES), 1)

    if with_q:
        u_ref[0] = proj[:, :conv_ch] * jax.nn.sigmoid(proj[:, conv_ch:o_cq])
        cqn = _rms(proj[:, o_cq:o_ckv], qag_ref[...], q_lora)
        qf = jnp.dot(cqn.astype(BF16), wq_ref[...], preferred_element_type=F32)
        q_cos = gq_ref[...] * cos
        q_sin = gqp_ref[...] * ssin
        bias_one = jnp.where(lane == BIAS_LANE, 1.0, 0.0)
        for hd in range(heads):
            qh = qf[:, hd * LANES:(hd + 1) * LANES]
            qp = qf[:, (heads + hd) * LANES:(heads + hd + 1) * LANES]
            r = lax.rsqrt(jnp.sum(qh * qh, axis=-1, keepdims=True) * (1.0 / QK_DIM) + EPS)
            qr = (qh * q_cos + qp * q_sin) * (r * Q_SCALE)
            q_ref[0, hd] = (qr + bias_one).astype(BF16)

    ckvn = _rms(proj[:, o_ckv:o_kr], kvag_ref[...], kv_lora).astype(BF16)
    kf = jnp.dot(ckvn, wk_ref[...], preferred_element_type=F32)
    vf = jnp.dot(ckvn, wv_ref[...], preferred_element_type=F32)
    kr = proj[:, o_kr:o_kr + LANES]
    krp = proj[:, o_kr + LANES:o_kr + 2 * LANES]
    gk = gk_ref[...]
    k_rot = kr * (gk * cos) + krp * (gkp_ref[...] * ssin)
    ss_rot = jnp.sum(kr * kr, axis=-1, keepdims=True)
    for hd in range(heads):
        kn = kf[:, hd * LANES:(hd + 1) * LANES]
        r = lax.rsqrt((jnp.sum(kn * kn, axis=-1, keepdims=True) + ss_rot) * (1.0 / QK_DIM) + EPS)
        kt_ref[0, hd, 0] = ((kn * gk + k_rot) * r + kbias_ref[...]).T.astype(BF16)
        one_hot = jnp.where(lane == _sum_lane(hd), 1.0, 0.0)
        v_ref[0, hd] = (vf[:, hd * LANES:(hd + 1) * LANES] + one_hot).astype(BF16)


def _sum_lane(head):
    return V_DIM if head % 2 == 0 else 0


def _even_in_call(x, sh, sc, gmix, w1, qag, wq, kvag, wk, wv, gq, gqp, gk, gkp, kbias, cos, ssin, *, heads, conv_ch,
                  q_lora, kv_lora, with_q, name):
    b, s, d = x.shape
    ts = _tile(s, 512)
    kb = _tile(s, ATTN_KB)
    r = kb // ts
    kern = functools.partial(_even_in_kernel, heads=heads, conv_ch=conv_ch, q_lora=q_lora, kv_lora=kv_lora,
                             with_q=with_q)
    full = lambda a: pl.BlockSpec(a.shape, lambda bi, i: (0,) * a.ndim)
    vec = pl.BlockSpec((1, 1, d), lambda bi, i: (bi, 0, 0))
    tab = pl.BlockSpec((ts, LANES), lambda bi, i: (i, 0))
    in_specs = [pl.BlockSpec((1, ts, d), lambda bi, i: (bi, i, 0)), vec, vec, full(gmix), full(w1), full(qag),
                full(wq), full(kvag), full(wk), full(wv), full(gq), full(gqp), full(gk), full(gkp), full(kbias),
                tab, tab]
    head_rows = pl.BlockSpec((1, heads, ts, LANES), lambda bi, i: (bi, 0, i, 0))
    out_specs = [pl.BlockSpec((1, heads, 1, LANES, ts), lambda bi, i: (bi, 0, i // r, 0, i % r)), head_rows]
    out_shape = [jax.ShapeDtypeStruct((b, heads, s // kb, LANES, kb), BF16),
                 jax.ShapeDtypeStruct((b, heads, s, LANES), BF16)]
    if with_q:
        out_specs = [pl.BlockSpec((1, ts, conv_ch), lambda bi, i: (bi, i, 0)), head_rows] + out_specs
        out_shape = [jax.ShapeDtypeStruct((b, s, conv_ch), F32),
                     jax.ShapeDtypeStruct((b, heads, s, LANES), BF16)] + out_shape
    return pl.pallas_call(
        kern, grid=(b, s // ts), in_specs=in_specs, out_specs=out_specs, out_shape=out_shape,
        compiler_params=_params(("parallel", "arbitrary")), name=name,
    )(x, sh, sc, gmix, w1, qag, wq, kvag, wk, wv, gq, gqp, gk, gkp, kbias, cos, ssin)


def _softmax_pv(h, s, v_blk, acc_ref, m_ref, online_max):
    if not online_max:
        acc_ref[h] += jnp.dot(jnp.exp2(s).astype(BF16), v_blk, preferred_element_type=F32)
        return
    nt = s.shape[1] // LANES
    tile = lambda t: s[:, t * LANES:(t + 1) * LANES]
    mx = tile(0)
    for t in range(1, nt):
        mx = jnp.maximum(mx, tile(t))
    m_old = m_ref[h]
    m_new = jnp.maximum(m_old, jnp.max(mx, axis=-1, keepdims=True))
    m_ref[h] = m_new
    pm = jnp.concatenate([jnp.exp2(tile(t) - m_new).astype(BF16) for t in range(nt)], axis=1)
    pv = jnp.dot(pm, v_blk, preferred_element_type=F32)
    acc_ref[h] = acc_ref[h] * jnp.exp2(m_old - m_new) + pv


def _conf_conv_rows(r0, buf_ref, win_ref, w_ref, b_ref, g_ref, beta_ref, a_ref):
    taps = w_ref.shape[0]
    first = CONF_HALO - (taps - 1) // 2
    r0 = pl.multiple_of(r0, CONF_ROWS)
    span = CONF_ROWS + (first + taps - 1) // SUBLANES * SUBLANES
    window = buf_ref[pl.ds(r0, span + SUBLANES), :]
    for j in range(1, SUBLANES):
        win_ref[j - 1] = window[j:j + span]
    acc = jnp.zeros((CONF_ROWS, buf_ref.shape[1]), F32)
    for k in range(taps):
        q, j = divmod(first + k, SUBLANES)
        if j == 0:
            tap = buf_ref[pl.ds(r0 + q * SUBLANES, CONF_ROWS), :]
        else:
            tap = win_ref[j - 1, q * SUBLANES:q * SUBLANES + CONF_ROWS, :]
        acc = acc + w_ref[k:k + 1, :] * tap
    u = acc + b_ref[...]
    mu = jnp.mean(u, axis=-1, keepdims=True)
    xc = u - mu
    y = xc * lax.rsqrt(jnp.mean(xc * xc, axis=-1, keepdims=True) + EPS) * g_ref[...] + beta_ref[...]
    a_ref[0, pl.ds(r0, CONF_ROWS), :] = _silu(y).astype(a_ref.dtype)


def _attn_kernel(q_ref, ktc_ref, vc_ref, kt_ref, v_ref, um_ref, up_ref, un_ref, cw_ref, cb_ref, lg_ref, lb_ref,
                 o_ref, a_ref, acc_ref, m_ref, s0_ref, s1_ref, c0_ref, c1_ref, buf_ref, win_ref, *, online_max):
    heads, nkb, kb = kt_ref.shape[1], kt_ref.shape[2], kt_ref.shape[4]
    tq = q_ref.shape[2]
    n_items = heads * nkb
    s_refs = (s0_ref, s1_ref)
    c_refs = (c0_ref, c1_ref)
    i, n = pl.program_id(1), pl.num_programs(1)
    acc_ref[...] = jnp.zeros(acc_ref.shape, F32)
    if online_max:
        m_ref[...] = jnp.full(m_ref.shape, NEG_BIG, F32)
    buf_ref[0:CONF_HALO] = jnp.where(i > 0, up_ref[0], 0.0)
    buf_ref[CONF_HALO:CONF_HALO + tq] = um_ref[0]
    buf_ref[CONF_HALO + tq:] = jnp.where(i < n - 1, un_ref[0], 0.0)

    def ctx_scores(h, slot):
        c_refs[slot][...] = jnp.dot(q_ref[0, h], ktc_ref[0, h, 0], preferred_element_type=F32)

    def ctx_fold(h, slot):
        _softmax_pv(h, c_refs[slot][...], vc_ref[0, h], acc_ref, m_ref, online_max)

    _software_pipeline(heads, ctx_scores, ctx_fold)

    def scores(t, slot):
        s_refs[slot][...] = jnp.dot(q_ref[0, t // nkb], kt_ref[0, t // nkb, t % nkb],
                                    preferred_element_type=F32)

    def fold(t, slot):
        rows = pl.ds(pl.multiple_of((t % nkb) * kb, kb), kb)
        _softmax_pv(t // nkb, s_refs[slot][...], v_ref[0, t // nkb, rows, :], acc_ref, m_ref, online_max)

    rows_per_body = tq // _pipeline_groups(n_items, ATTN_GROUP)

    def conf_rows(g):
        for c in range(rows_per_body // CONF_ROWS):
            _conf_conv_rows(g * rows_per_body + c * CONF_ROWS, buf_ref, win_ref, cw_ref, cb_ref, lg_ref, lb_ref,
                            a_ref)

    _software_pipeline(n_items, scores, fold, group=ATTN_GROUP, filler=conf_rows)

    lane = lax.broadcasted_iota(jnp.int32, (1, LANES), 1)
    for p in range(heads // 2):
        even, odd = acc_ref[2 * p], acc_ref[2 * p + 1]
        l_even = even[:, _sum_lane(0):_sum_lane(0) + 1]
        l_odd = odd[:, _sum_lane(1):_sum_lane(1) + 1]
        o_ref[0, :, p * LANES:(p + 1) * LANES] = jnp.where(lane < V_DIM, even / l_even, odd / l_odd).astype(
            o_ref.dtype)


def _attn_call(q, ktc, vc, kt, v, u, cw, cb, lg, lb, *, online_max):
    b, heads, s, _ = q.shape
    ch = u.shape[2]
    tq = _tile(s, 512)
    kb = kt.shape[4]
    taps = cw.shape[0]
    groups = _pipeline_groups(heads * kt.shape[2], ATTN_GROUP)
    assert (taps - 1) // 2 <= CONF_HALO and tq % (groups * CONF_ROWS) == 0
    hb, nh = tq // CONF_HALO, s // CONF_HALO
    span = CONF_ROWS + (CONF_HALO - (taps - 1) // 2 + taps - 1) // SUBLANES * SUBLANES
    resident = lambda a: pl.BlockSpec((1,) + a.shape[1:], lambda bi, i: (bi,) + (0,) * (a.ndim - 1),
                                      pipeline_mode=pl.Buffered(1))
    full = lambda a: pl.BlockSpec(a.shape, lambda bi, i: (0,) * a.ndim)
    rows = lambda c: pl.BlockSpec((1, tq, c), lambda bi, i: (bi, i, 0))
    return pl.pallas_call(
        functools.partial(_attn_kernel, online_max=online_max), grid=(b, s // tq),
        in_specs=[pl.BlockSpec((1, heads, tq, LANES), lambda bi, i: (bi, 0, i, 0)),
                  resident(ktc), resident(vc), resident(kt), resident(v),
                  rows(ch),
                  pl.BlockSpec((1, CONF_HALO, ch), lambda bi, i: (bi, jnp.maximum(i * hb - 1, 0), 0)),
                  pl.BlockSpec((1, CONF_HALO, ch), lambda bi, i: (bi, jnp.minimum((i + 1) * hb, nh - 1), 0)),
                  full(cw), full(cb), full(lg), full(lb)],
        out_specs=[rows(heads * V_DIM), rows(ch)],
        out_shape=[jax.ShapeDtypeStruct((b, s, heads * V_DIM), BF16), jax.ShapeDtypeStruct((b, s, ch), BF16)],
        scratch_shapes=[pltpu.VMEM((heads, tq, LANES), F32), pltpu.VMEM((heads, tq, LANES), F32),
                        pltpu.VMEM((tq, kb), F32), pltpu.VMEM((tq, kb), F32),
                        pltpu.VMEM((tq, ktc.shape[4]), F32), pltpu.VMEM((tq, ktc.shape[4]), F32),
                        pltpu.VMEM((tq + 2 * CONF_HALO, ch), F32), pltpu.VMEM((SUBLANES - 1, span, ch), F32)],
        compiler_params=_params(("parallel", "arbitrary")), name="attn_online" if online_max else "attn",
    )(q, ktc, vc, kt, v, u, u, u, cw, cb, lg, lb)


def _out_proj_kernel(x_ref, a_ref, t_ref, g_ref, wa_ref, wt_ref, o_ref):
    y = jnp.dot(a_ref[0], wa_ref[...], preferred_element_type=F32)
    y = y + jnp.dot(t_ref[0], wt_ref[...], preferred_element_type=F32)
    o_ref[0] = x_ref[0] + g_ref[0] * y


def _out_proj_call(x, a, att, g, wa, wt):
    b, s, d = x.shape
    ts = _tile(s, 1024)
    row = lambda c: pl.BlockSpec((1, ts, c), lambda bi, i: (bi, i, 0))
    full = lambda arr: pl.BlockSpec(arr.shape, lambda bi, i: (0,) * arr.ndim)
    return pl.pallas_call(
        _out_proj_kernel, grid=(b, s // ts),
        in_specs=[row(d), row(a.shape[2]), row(att.shape[2]),
                  pl.BlockSpec((1, 1, d), lambda bi, i: (bi, 0, 0)), full(wa), full(wt)],
        out_specs=row(d), out_shape=jax.ShapeDtypeStruct((b, s, d), F32),
        compiler_params=_params(("parallel", "arbitrary")), name="out_proj",
    )(x, a, att, g, wa, wt)


def _halo_hidden(xm_ref, xp_ref, xn_ref, sh_ref, sc_ref, ng_ref):
    i, n = pl.program_id(1), pl.num_programs(1)
    ts, d = xm_ref.shape[1], xm_ref.shape[2]
    xe = jnp.concatenate([xp_ref[0], xm_ref[0], xn_ref[0]], axis=0)
    h = _rms(xe, ng_ref[...], d) * (1.0 + sc_ref[0]) + sh_ref[0]
    row = lax.broadcasted_iota(jnp.int32, (ts + 2 * HALO, 1), 0)
    lo = jnp.where(i > 0, 0, HALO)
    hi = jnp.where(i < n - 1, ts + 2 * HALO, ts + HALO)
    valid = jnp.logical_and(row >= lo, row < hi)
    return jnp.where(valid, h, 0.0).astype(BF16)


def _conv3(z, w, ts):
    rows = z.shape[0]
    y = w[0:1] * pltpu.roll(z, 1, 0) + w[1:2] * z + w[2:3] * pltpu.roll(z, rows - 1, 0)
    return y[HALO:HALO + ts]


def _halo_specs(ts, s, d):
    hb, nh = ts // HALO, s // HALO
    return [pl.BlockSpec((1, ts, d), lambda bi, i: (bi, i, 0)),
            pl.BlockSpec((1, HALO, d), lambda bi, i: (bi, jnp.maximum(i * hb - 1, 0), 0)),
            pl.BlockSpec((1, HALO, d), lambda bi, i: (bi, jnp.minimum((i + 1) * hb, nh - 1), 0))]


def _gated_conv_kernel(xm_ref, xp_ref, xn_ref, sh_ref, sc_ref, g_ref, ng_ref, win_ref, cw_ref, cb_ref, wout_ref,
                       o_ref, acc_ref, h_ref, t0_ref, t1_ref, *, is_ffn):
    ts = xm_ref.shape[1]
    c = wout_ref.shape[1]
    t_refs = (t0_ref, t1_ref)
    h_ref[...] = _halo_hidden(xm_ref, xp_ref, xn_ref, sh_ref, sc_ref, ng_ref)
    acc_ref[...] = jnp.zeros(acc_ref.shape, F32)

    def project(k, slot):
        t_refs[slot][...] = jnp.dot(h_ref[...], win_ref[k], preferred_element_type=F32)

    def mix(k, slot):
        t_ref = t_refs[slot]
        tile_rows = pl.ds(HALO, ts)
        if is_ffn:
            conv = _conv3(t_ref[:, 0:c], cw_ref[k], ts) + cb_ref[k]
            y = _silu(conv) * t_ref[tile_rows, c:2 * c]
        else:
            conv = _conv3(t_ref[:, c:2 * c] * t_ref[:, 2 * c:3 * c], cw_ref[k], ts) + cb_ref[k]
            y = t_ref[tile_rows, 0:c] * conv
        acc_ref[...] += jnp.dot(y.astype(BF16), wout_ref[k], preferred_element_type=F32)

    _software_pipeline(win_ref.shape[0], project, mix, unroll=True)
    o_ref[0] = xm_ref[0] + g_ref[0] * acc_ref[...]


def _gated_conv_call(x, sh, sc, g, ng, win, cw, cb, wout, *, is_ffn, name):
    b, s, d = x.shape
    ts = _tile(s, 512)
    full = lambda a: pl.BlockSpec(a.shape, lambda bi, i: (0,) * a.ndim)
    vec = pl.BlockSpec((1, 1, d), lambda bi, i: (bi, 0, 0))
    t_buf = pltpu.VMEM((ts + 2 * HALO, win.shape[2]), F32)
    return pl.pallas_call(
        functools.partial(_gated_conv_kernel, is_ffn=is_ffn), grid=(b, s // ts),
        in_specs=_halo_specs(ts, s, d) + [vec, vec, vec, full(ng), full(win), full(cw), full(cb), full(wout)],
        out_specs=pl.BlockSpec((1, ts, d), lambda bi, i: (bi, i, 0)),
        out_shape=jax.ShapeDtypeStruct((b, s, d), F32),
        scratch_shapes=[pltpu.VMEM((ts, d), F32), pltpu.VMEM((ts + 2 * HALO, d), BF16), t_buf, t_buf],
        compiler_params=_params(("parallel", "arbitrary")), name=name,
    )(x, x, x, sh, sc, g, ng, win, cw, cb, wout)


def _rope_partner():
    quarter = QK_ROPE // 4
    partner = list(range(LANES))
    for g in range(QK_NOPE, QK_DIM, 2 * quarter):
        for l in range(g, g + quarter):
            partner[l], partner[l + quarter] = l + quarter, l
    return jnp.array(partner, jnp.int32)


def _rope_tables(n_tokens):
    half = QK_ROPE // 2
    quarter = half // 2
    inv = ROPE_THETA ** (-jnp.arange(0, half, 2, dtype=F32) / half)
    lane = jnp.arange(LANES)
    on = jnp.logical_and(lane >= QK_NOPE, lane < QK_DIM)[None, :]
    r = jnp.clip(lane - QK_NOPE, 0, QK_ROPE - 1)
    use_col = (r >= half)[None, :]
    second = ((r % half) >= quarter)[None, :]
    pos = jnp.arange(n_tokens)
    p = jnp.where(use_col, (pos % GRID_W)[:, None], (pos // GRID_W)[:, None]).astype(F32)
    ang = p * inv[r % quarter][None, :]
    sin = jnp.sin(ang)
    return jnp.where(on, jnp.cos(ang), 1.0), jnp.where(on, jnp.where(second, sin, -sin), 0.0)


def _pad_heads(w, heads, width):
    k = w.shape[0]
    w = w.reshape(k, heads, width)
    return jnp.pad(w, ((0, 0), (0, 0), (0, LANES - width))).reshape(k, heads * LANES)


def _chunk_cols(w, parts, chunk):
    d = w.shape[0]
    n = w.shape[1] // parts
    w = w.reshape(d, parts, n // chunk, chunk)
    return jnp.transpose(w, (2, 0, 1, 3)).reshape(n // chunk, d, parts * chunk)


def _chunk_vec(v, chunk):
    k, n = v.shape
    return jnp.transpose(v.reshape(k, n // chunk, chunk), (1, 0, 2))


def kernel(x, c, ctx, c_ctx, ada_w, ada_b, norm_mix_g, norm_ffn_g, ffn_w_up, ffn_conv_w, ffn_conv_b, ffn_w_down,
           ev_w_in, ev_conv_w, ev_conv_b, ev_ln_g, ev_ln_b, ev_qa_norm_g, ev_w_uq, ev_kva_norm_g, ev_w_ukv,
           ev_q_norm_g, ev_k_norm_g, ev_w_out, od_w_in, od_conv_w, od_conv_b, od_w_out):
    b, s, d = x.shape
    n_ctx = ctx.shape[1]
    depth = ada_w.shape[0]
    assert depth == 2, "kernel implements the two-layer (even, odd) block"
    conv_ch = ev_conv_w.shape[2]
    q_lora = ev_qa_norm_g.shape[1]
    kv_lora = ev_kva_norm_g.shape[1]
    heads = ev_w_uq.shape[2] // QK_DIM
    d_mix = od_conv_w.shape[2]
    ffn_dim = ffn_conv_w.shape[2]
    assert s % GRID_W == 0 and s % LANES == 0 and n_ctx % LANES == 0 and heads % 2 == 0
    assert ev_w_in.shape[2] == 2 * conv_ch + q_lora + kv_lora + QK_ROPE

    rows = -(-(b + 1) // HALO) * HALO
    cvec = jnp.concatenate([c, c_ctx[None, :], jnp.zeros((rows - b - 1, d), F32)], axis=0)
    mod = _mod_call(cvec, ada_w, ada_b)
    lat = lambda l, k: mod[l, :b, None, k * d:(k + 1) * d]
    ctxm = lambda l, k: jnp.broadcast_to(mod[l, b:b + 1, None, k * d:(k + 1) * d], (b, 1, d))

    w_in = ev_w_in[0]
    o_kr = 2 * conv_ch + q_lora + kv_lora
    partner = _rope_partner()
    kr_tile = jnp.pad(w_in[:, o_kr:], ((0, 0), (QK_NOPE, LANES - QK_DIM)))
    w1 = jnp.concatenate([w_in[:, :o_kr], kr_tile, kr_tile[:, partner]], axis=1).astype(BF16)
    wq = _pad_heads(ev_w_uq[0], heads, QK_DIM)
    wq_partner = wq.reshape(q_lora, heads, LANES)[:, :, partner].reshape(q_lora, heads * LANES)
    wq = jnp.concatenate([wq, wq_partner], axis=1).astype(BF16)
    wkv = ev_w_ukv[0].reshape(kv_lora, heads, QK_NOPE + V_DIM)
    wk = jnp.pad(wkv[:, :, :QK_NOPE], ((0, 0), (0, 0), (0, LANES - QK_NOPE))).reshape(kv_lora, heads * LANES)
    wk = wk.astype(BF16)
    wv3 = wkv[:, :, QK_NOPE:]
    zv = jnp.zeros_like(wv3)
    even_head = (jnp.arange(heads) % 2 == 0)[None, :, None]
    wv = jnp.where(even_head, jnp.concatenate([wv3, zv], -1), jnp.concatenate([zv, wv3], -1))
    wv = wv.reshape(kv_lora, heads * LANES).astype(BF16)
    gq = jnp.pad(ev_q_norm_g[0], (0, LANES - QK_DIM))[None, :]
    gk = jnp.pad(ev_k_norm_g[0], (0, LANES - QK_DIM))[None, :]
    even_kw = dict(heads=heads, conv_ch=conv_ch, q_lora=q_lora, kv_lora=kv_lora)
    score_bound = QK_DIM * jnp.max(jnp.abs(ev_q_norm_g[0])) * jnp.max(jnp.abs(ev_k_norm_g[0])) * Q_SCALE
    static_shift = score_bound < MAX_STATIC_SHIFT
    kbias = jnp.where(jnp.logical_and(static_shift, jnp.arange(LANES) == BIAS_LANE), -score_bound, 0.0)[None, :]
    even_w = (norm_mix_g[0:1], w1, ev_qa_norm_g, wq, ev_kva_norm_g, wk, wv, gq, gq[:, partner], gk, gk[:, partner],
              kbias)

    cos, ssin = _rope_tables(s)
    ident = (jnp.ones((n_ctx, LANES), F32), jnp.zeros((n_ctx, LANES), F32))

    ktc, vc = _even_in_call(ctx, ctxm(0, 0), ctxm(0, 1), *even_w, *ident, with_q=False, name="even_in_ctx",
                            **even_kw)
    u, q, kt, v = _even_in_call(x, lat(0, 0), lat(0, 1), *even_w, cos, ssin, with_q=True, name="even_in",
                                **even_kw)
    att, a = lax.cond(static_shift, functools.partial(_attn_call, online_max=False),
                      functools.partial(_attn_call, online_max=True),
                      q, ktc, vc, kt, v, u, ev_conv_w[0], ev_conv_b, ev_ln_g, ev_ln_b)
    w_out = ev_w_out[0].astype(BF16)
    x = _out_proj_call(x, a, att, lat(0, 2), w_out[:conv_ch], w_out[conv_ch:])

    fc = MXU_WIDTH if ffn_dim % MXU_WIDTH == 0 else LANES

    def ffn(xin, l):
        wup = _chunk_cols(ffn_w_up[l], 2, fc).astype(BF16)
        wdn = ffn_w_down[l].reshape(ffn_dim // fc, fc, d).astype(BF16)
        return _gated_conv_call(xin, lat(l, 3), lat(l, 4), lat(l, 5), norm_ffn_g[l:l + 1], wup,
                                _chunk_vec(ffn_conv_w[l], fc), _chunk_vec(ffn_conv_b[l:l + 1], fc), wdn,
                                is_ffn=True, name="ffn")

    x = ffn(x, 0)

    mc = MXU_WIDTH if d_mix % MXU_WIDTH == 0 else LANES
    win = _chunk_cols(od_w_in[0], 3, mc).astype(BF16)
    wout = od_w_out[0].reshape(d_mix // mc, mc, d).astype(BF16)
    x = _gated_conv_call(x, lat(1, 0), lat(1, 1), lat(1, 2), norm_mix_g[1:2], win, _chunk_vec(od_conv_w[0], mc),
                         _chunk_vec(od_conv_b, mc), wout, is_ffn=False, name="odd_mix")
    return ffn(x, 1)
```

```python
import functools
import math

import jax
import jax.numpy as jnp
from jax import lax
from jax.experimental import pallas as pl
from jax.experimental.pallas import tpu as pltpu

F32 = jnp.float32
BF16 = jnp.bfloat16

GRID_W = 64
QK_NOPE = 64
QK_ROPE = 32
V_DIM = 64
QK_DIM = QK_NOPE + QK_ROPE
BIAS_LANE = QK_DIM
ROPE_THETA = 10000.0
EPS = 1e-6
SM_SCALE = QK_DIM ** -0.5
Q_SCALE = SM_SCALE * math.log2(math.e)
N_MOD = 6

LANES = 128
SUBLANES = 8
MXU_WIDTH = 256
VMEM_LIMIT = 60000 * 1024

ATTN_KB = 1024
ATTN_GROUP = 4
CONF_ROWS = 32
HALO = SUBLANES
CONF_HALO = 16
NEG_BIG = -1e30
MAX_STATIC_SHIFT = 40.0


def _tile(n, pref):
    t = min(n, pref)
    while n % t:
        t //= 2
    return t


def _rms(x, g, n):
    ms = jnp.sum(x * x, axis=-1, keepdims=True) * (1.0 / n)
    return x * lax.rsqrt(ms + EPS) * g


def _silu(x):
    return x * jax.nn.sigmoid(x)


def _params(sem):
    return pltpu.CompilerParams(dimension_semantics=sem, vmem_limit_bytes=VMEM_LIMIT)


def _software_pipeline(n, produce, consume, *, group=2, unroll=False, filler=None):
    assert group % 2 == 0
    produce(0, 0)
    n_loop = 0 if unroll else (n - 1) // group

    def body(u, carry):
        for j in range(group):
            produce(group * u + j + 1, (j + 1) % 2)
            consume(group * u + j, j % 2)
        if filler is not None:
            filler(u)
        return carry
    if n_loop:
        lax.fori_loop(0, n_loop, body, 0)
    for t in range(group * n_loop, n):
        if t + 1 < n:
            produce(t + 1, (t + 1) % 2)
        consume(t, t % 2)
    if filler is not None:
        filler(n_loop)


def _pipeline_groups(n, group):
    return (n - 1) // group + 1


def _mod_kernel(c_ref, w_ref, b_ref, o_ref):
    s = _silu(c_ref[...])
    o_ref[0] = jnp.dot(s, w_ref[0], preferred_element_type=F32,
                       precision=lax.Precision.HIGHEST) + b_ref[0]


def _mod_call(cvec, ada_w, ada_b):
    depth, d, nd = ada_w.shape
    rows = cvec.shape[0]
    tn = _tile(nd, 1536)
    return pl.pallas_call(
        _mod_kernel,
        grid=(depth, nd // tn),
        in_specs=[
            pl.BlockSpec((rows, d), lambda l, j: (0, 0)),
            pl.BlockSpec((1, d, tn), lambda l, j: (l, 0, j)),
            pl.BlockSpec((1, 1, tn), lambda l, j: (l, 0, j)),
        ],
        out_specs=pl.BlockSpec((1, rows, tn), lambda l, j: (l, 0, j)),
        out_shape=jax.ShapeDtypeStruct((depth, rows, nd), F32),
        compiler_params=_params(("arbitrary", "arbitrary")),
        name="mod",
    )(cvec, ada_w, ada_b.reshape(depth, 1, nd))


def _even_in_kernel(x_ref, sh_ref, sc_ref, gmix_ref, w1_ref, qag_ref, wq_ref, kvag_ref, wk_ref, wv_ref,
                    gq_ref, gqp_ref, gk_ref, gkp_ref, kbias_ref, cos_ref, ssin_ref, *out_refs, heads, conv_ch, q_lora,
                    kv_lora, with_q):
    if with_q:
        u_ref, q_ref, kt_ref, v_ref = out_refs
    else:
        kt_ref, v_ref = out_refs
    ts, d = x_ref.shape[1], x_ref.shape[2]
    x = x_ref[0]
    h = _rms(x, gmix_ref[...], d) * (1.0 + sc_ref[0]) + sh_ref[0]
    proj = jnp.dot(h.astype(BF16), w1_ref[...], preferred_element_type=F32)
    o_cq = 2 * conv_ch
    o_ckv = o_cq + q_lora
    o_kr = o_ckv + kv_lora
    cos, ssin = cos_ref[...], ssin_ref[...]
    lane = lax.broadcasted_iota(jnp.int32, (1, LANES), 1)

    if with_q:
        u_ref[0] = proj[:, :conv_ch] * jax.nn.sigmoid(proj[:, conv_ch:o_cq])
        cqn = _rms(proj[:, o_cq:o_ckv], qag_ref[...], q_lora)
        qf = jnp.dot(cqn.astype(BF16), wq_ref[...], preferred_element_type=F32)
        q_cos = gq_ref[...] * cos
        q_sin = gqp_ref[...] * ssin
        bias_one = jnp.where(lane == BIAS_LANE, 1.0, 0.0)
        for hd in range(heads):
            qh = qf[:, hd * LANES:(hd + 1) * LANES]
            qp = qf[:, (heads + hd) * LANES:(heads + hd + 1) * LANES]
            r = lax.rsqrt(jnp.sum(qh * qh, axis=-1, keepdims=True) * (1.0 / QK_DIM) + EPS)
            qr = (qh * q_cos + qp * q_sin) * (r * Q_SCALE)
            q_ref[0, hd] = (qr + bias_one).astype(BF16)

    ckvn = _rms(proj[:, o_ckv:o_kr], kvag_ref[...], kv_lora).astype(BF16)
    kf = jnp.dot(ckvn, wk_ref[...], preferred_element_type=F32)
    vf = jnp.dot(ckvn, wv_ref[...], preferred_element_type=F32)
    kr = proj[:, o_kr:o_kr + LANES]
    krp = proj[:, o_kr + LANES:o_kr + 2 * LANES]
    gk = gk_ref[...]
    k_rot = kr * (gk * cos) + krp * (gkp_ref[...] * ssin)
    ss_rot = jnp.sum(kr * kr, axis=-1, keepdims=True)
    for hd in range(heads):
        kn = kf[:, hd * LANES:(hd + 1) * LANES]
        r = lax.rsqrt((jnp.sum(kn * kn, axis=-1, keepdims=True) + ss_rot) * (1.0 / QK_DIM) + EPS)
        kt_ref[0, hd, 0] = ((kn * gk + k_rot) * r + kbias_ref[...]).T.astype(BF16)
        one_hot = jnp.where(lane == _sum_lane(hd), 1.0, 0.0)
        v_ref[0, hd] = (vf[:, hd * LANES:(hd + 1) * LANES] + one_hot).astype(BF16)


def _sum_lane(head):
    return V_DIM if head % 2 == 0 else 0


def _even_in_call(x, sh, sc, gmix, w1, qag, wq, kvag, wk, wv, gq, gqp, gk, gkp, kbias, cos, ssin, *, heads, conv_ch,
                  q_lora, kv_lora, with_q, name):
    b, s, d = x.shape
    ts = _tile(s, 512)
    kb = _tile(s, ATTN_KB)
    r = kb // ts
    kern = functools.partial(_even_in_kernel, heads=heads, conv_ch=conv_ch, q_lora=q_lora, kv_lora=kv_lora,
                             with_q=with_q)
    full = lambda a: pl.BlockSpec(a.shape, lambda bi, i: (0,) * a.ndim)
    vec = pl.BlockSpec((1, 1, d), lambda bi, i: (bi, 0, 0))
    tab = pl.BlockSpec((ts, LANES), lambda bi, i: (i, 0))
    in_specs = [pl.BlockSpec((1, ts, d), lambda bi, i: (bi, i, 0)), vec, vec, full(gmix), full(w1), full(qag),
                full(wq), full(kvag), full(wk), full(wv), full(gq), full(gqp), full(gk), full(gkp), full(kbias),
                tab, tab]
    head_rows = pl.BlockSpec((1, heads, ts, LANES), lambda bi, i: (bi, 0, i, 0))
    out_specs = [pl.BlockSpec((1, heads, 1, LANES, ts), lambda bi, i: (bi, 0, i // r, 0, i % r)), head_rows]
    out_shape = [jax.ShapeDtypeStruct((b, heads, s // kb, LANES, kb), BF16),
                 jax.ShapeDtypeStruct((b, heads, s, LANES), BF16)]
    if with_q:
        out_specs = [pl.BlockSpec((1, ts, conv_ch), lambda bi, i: (bi, i, 0)), head_rows] + out_specs
        out_shape = [jax.ShapeDtypeStruct((b, s, conv_ch), F32),
                     jax.ShapeDtypeStruct((b, heads, s, LANES), BF16)] + out_shape
    return pl.pallas_call(
        kern, grid=(b, s // ts), in_specs=in_specs, out_specs=out_specs, out_shape=out_shape,
        compiler_params=_params(("parallel", "arbitrary")), name=name,
    )(x, sh, sc, gmix, w1, qag, wq, kvag, wk, wv, gq, gqp, gk, gkp, kbias, cos, ssin)


def _softmax_pv(h, s, v_blk, acc_ref, m_ref, online_max):
    if not online_max:
        acc_ref[h] += jnp.dot(jnp.exp2(s).astype(BF16), v_blk, preferred_element_type=F32)
        return
    nt = s.shape[1] // LANES
    tile = lambda t: s[:, t * LANES:(t + 1) * LANES]
    mx = tile(0)
    for t in range(1, nt):
        mx = jnp.maximum(mx, tile(t))
    m_old = m_ref[h]
    m_new = jnp.maximum(m_old, jnp.max(mx, axis=-1, keepdims=True))
    m_ref[h] = m_new
    pm = jnp.concatenate([jnp.exp2(tile(t) - m_new).astype(BF16) for t in range(nt)], axis=1)
    pv = jnp.dot(pm, v_blk, preferred_element_type=F32)
    acc_ref[h] = acc_ref[h] * jnp.exp2(m_old - m_new) + pv


def _conf_conv_rows(r0, buf_ref, win_ref, w_ref, b_ref, g_ref, beta_ref, a_ref):
    taps = w_ref.shape[0]
    first = CONF_HALO - (taps - 1) // 2
    r0 = pl.multiple_of(r0, CONF_ROWS)
    span = CONF_ROWS + (first + taps - 1) // SUBLANES * SUBLANES
    window = buf_ref[pl.ds(r0, span + SUBLANES), :]
    for j in range(1, SUBLANES):
        win_ref[j - 1] = window[j:j + span]
    acc = jnp.zeros((CONF_ROWS, buf_ref.shape[1]), F32)
    for k in range(taps):
        q, j = divmod(first + k, SUBLANES)
        if j == 0:
            tap = buf_ref[pl.ds(r0 + q * SUBLANES, CONF_ROWS), :]
        else:
            tap = win_ref[j - 1, q * SUBLANES:q * SUBLANES + CONF_ROWS, :]
        acc = acc + w_ref[k:k + 1, :] * tap
    u = acc + b_ref[...]
    mu = jnp.mean(u, axis=-1, keepdims=True)
    xc = u - mu
    y = xc * lax.rsqrt(jnp.mean(xc * xc, axis=-1, keepdims=True) + EPS) * g_ref[...] + beta_ref[...]
    a_ref[0, pl.ds(r0, CONF_ROWS), :] = _silu(y).astype(a_ref.dtype)


def _attn_kernel(q_ref, ktc_ref, vc_ref, kt_ref, v_ref, um_ref, up_ref, un_ref, cw_ref, cb_ref, lg_ref, lb_ref,
                 o_ref, a_ref, acc_ref, m_ref, s0_ref, s1_ref, c0_ref, c1_ref, buf_ref, win_ref, *, online_max):
    heads, nkb, kb = kt_ref.shape[1], kt_ref.shape[2], kt_ref.shape[4]
    tq = q_ref.shape[2]
    n_items = heads * nkb
    s_refs = (s0_ref, s1_ref)
    c_refs = (c0_ref, c1_ref)
    i, n = pl.program_id(1), pl.num_programs(1)
    acc_ref[...] = jnp.zeros(acc_ref.shape, F32)
    if online_max:
        m_ref[...] = jnp.full(m_ref.shape, NEG_BIG, F32)
    buf_ref[0:CONF_HALO] = jnp.where(i > 0, up_ref[0], 0.0)
    buf_ref[CONF_HALO:CONF_HALO + tq] = um_ref[0]
    buf_ref[CONF_HALO + tq:] = jnp.where(i < n - 1, un_ref[0], 0.0)

    def ctx_scores(h, slot):
        c_refs[slot][...] = jnp.dot(q_ref[0, h], ktc_ref[0, h, 0], preferred_element_type=F32)

    def ctx_fold(h, slot):
        _softmax_pv(h, c_refs[slot][...], vc_ref[0, h], acc_ref, m_ref, online_max)

    _software_pipeline(heads, ctx_scores, ctx_fold)

    def scores(t, slot):
        s_refs[slot][...] = jnp.dot(q_ref[0, t // nkb], kt_ref[0, t // nkb, t % nkb],
                                    preferred_element_type=F32)

    def fold(t, slot):
        rows = pl.ds(pl.multiple_of((t % nkb) * kb, kb), kb)
        _softmax_pv(t // nkb, s_refs[slot][...], v_ref[0, t // nkb, rows, :], acc_ref, m_ref, online_max)

    rows_per_body = tq // _pipeline_groups(n_items, ATTN_GROUP)

    def conf_rows(g):
        for c in range(rows_per_body // CONF_ROWS):
            _conf_conv_rows(g * rows_per_body + c * CONF_ROWS, buf_ref, win_ref, cw_ref, cb_ref, lg_ref, lb_ref,
                            a_ref)

    _software_pipeline(n_items, scores, fold, group=ATTN_GROUP, filler=conf_rows)

    lane = lax.broadcasted_iota(jnp.int32, (1, LANES), 1)
    for p in range(heads // 2):
        even, odd = acc_ref[2 * p], acc_ref[2 * p + 1]
        l_even = even[:, _sum_lane(0):_sum_lane(0) + 1]
        l_odd = odd[:, _sum_lane(1):_sum_lane(1) + 1]
        o_ref[0, :, p * LANES:(p + 1) * LANES] = jnp.where(lane < V_DIM, even / l_even, odd / l_odd).astype(
            o_ref.dtype)


def _attn_call(q, ktc, vc, kt, v, u, cw, cb, lg, lb, *, online_max):
    b, heads, s, _ = q.shape
    ch = u.shape[2]
    tq = _tile(s, 512)
    kb = kt.shape[4]
    taps = cw.shape[0]
    groups = _pipeline_groups(heads * kt.shape[2], ATTN_GROUP)
    assert (taps - 1) // 2 <= CONF_HALO and tq % (groups * CONF_ROWS) == 0
    hb, nh = tq // CONF_HALO, s // CONF_HALO
    span = CONF_ROWS + (CONF_HALO - (taps - 1) // 2 + taps - 1) // SUBLANES * SUBLANES
    resident = lambda a: pl.BlockSpec((1,) + a.shape[1:], lambda bi, i: (bi,) + (0,) * (a.ndim - 1),
                                      pipeline_mode=pl.Buffered(1))
    full = lambda a: pl.BlockSpec(a.shape, lambda bi, i: (0,) * a.ndim)
    rows = lambda c: pl.BlockSpec((1, tq, c), lambda bi, i: (bi, i, 0))
    return pl.pallas_call(
        functools.partial(_attn_kernel, online_max=online_max), grid=(b, s // tq),
        in_specs=[pl.BlockSpec((1, heads, tq, LANES), lambda bi, i: (bi, 0, i, 0)),
                  resident(ktc), resident(vc), resident(kt), resident(v),
                  rows(ch),
                  pl.BlockSpec((1, CONF_HALO, ch), lambda bi, i: (bi, jnp.maximum(i * hb - 1, 0), 0)),
                  pl.BlockSpec((1, CONF_HALO, ch), lambda bi, i: (bi, jnp.minimum((i + 1) * hb, nh - 1), 0)),
                  full(cw), full(cb), full(lg), full(lb)],
        out_specs=[rows(heads * V_DIM), rows(ch)],
        out_shape=[jax.ShapeDtypeStruct((b, s, heads * V_DIM), BF16), jax.ShapeDtypeStruct((b, s, ch), BF16)],
        scratch_shapes=[pltpu.VMEM((heads, tq, LANES), F32), pltpu.VMEM((heads, tq, LANES), F32),
                        pltpu.VMEM((tq, kb), F32), pltpu.VMEM((tq, kb), F32),
                        pltpu.VMEM((tq, ktc.shape[4]), F32), pltpu.VMEM((tq, ktc.shape[4]), F32),
                        pltpu.VMEM((tq + 2 * CONF_HALO, ch), F32), pltpu.VMEM((SUBLANES - 1, span, ch), F32)],
        compiler_params=_params(("parallel", "arbitrary")), name="attn_online" if online_max else "attn",
    )(q, ktc, vc, kt, v, u, u, u, cw, cb, lg, lb)


def _out_proj_kernel(x_ref, a_ref, t_ref, g_ref, wa_ref, wt_ref, o_ref):
    y = jnp.dot(a_ref[0], wa_ref[...], preferred_element_type=F32)
    y = y + jnp.dot(t_ref[0], wt_ref[...], preferred_element_type=F32)
    o_ref[0] = x_ref[0] + g_ref[0] * y


def _out_proj_call(x, a, att, g, wa, wt):
    b, s, d = x.shape
    ts = _tile(s, 1024)
    row = lambda c: pl.BlockSpec((1, ts, c), lambda bi, i: (bi, i, 0))
    full = lambda arr: pl.BlockSpec(arr.shape, lambda bi, i: (0,) * arr.ndim)
    return pl.pallas_call(
        _out_proj_kernel, grid=(b, s // ts),
        in_specs=[row(d), row(a.shape[2]), row(att.shape[2]),
                  pl.BlockSpec((1, 1, d), lambda bi, i: (bi, 0, 0)), full(wa), full(wt)],
        out_specs=row(d), out_shape=jax.ShapeDtypeStruct((b, s, d), F32),
        compiler_params=_params(("parallel", "arbitrary")), name="out_proj",
    )(x, a, att, g, wa, wt)


def _halo_hidden(xm_ref, xp_ref, xn_ref, sh_ref, sc_ref, ng_ref):
    i, n = pl.program_id(1), pl.num_programs(1)
    ts, d = xm_ref.shape[1], xm_ref.shape[2]
    xe = jnp.concatenate([xp_ref[0], xm_ref[0], xn_ref[0]], axis=0)
    h = _rms(xe, ng_ref[...], d) * (1.0 + sc_ref[0]) + sh_ref[0]
    row = lax.broadcasted_iota(jnp.int32, (ts + 2 * HALO, 1), 0)
    lo = jnp.where(i > 0, 0, HALO)
    hi = jnp.where(i < n - 1, ts + 2 * HALO, ts + HALO)
    valid = jnp.logical_and(row >= lo, row < hi)
    return jnp.where(valid, h, 0.0).astype(BF16)


def _conv3(z, w, ts):
    rows = z.shape[0]
    y = w[0:1] * pltpu.roll(z, 1, 0) + w[1:2] * z + w[2:3] * pltpu.roll(z, rows - 1, 0)
    return y[HALO:HALO + ts]


def _halo_specs(ts, s, d):
    hb, nh = ts // HALO, s // HALO
    return [pl.BlockSpec((1, ts, d), lambda bi, i: (bi, i, 0)),
            pl.BlockSpec((1, HALO, d), lambda bi, i: (bi, jnp.maximum(i * hb - 1, 0), 0)),
            pl.BlockSpec((1, HALO, d), lambda bi, i: (bi, jnp.minimum((i + 1) * hb, nh - 1), 0))]


def _gated_conv_kernel(xm_ref, xp_ref, xn_ref, sh_ref, sc_ref, g_ref, ng_ref, win_ref, cw_ref, cb_ref, wout_ref,
                       o_ref, acc_ref, h_ref, t0_ref, t1_ref, *, is_ffn, c):
    ts = xm_ref.shape[1]
    f = wout_ref.shape[0]
    parts = win_ref.shape[1] // f
    t_refs = (t0_ref, t1_ref)
    h_ref[...] = _halo_hidden(xm_ref, xp_ref, xn_ref, sh_ref, sc_ref, ng_ref)
    acc_ref[...] = jnp.zeros(acc_ref.shape, F32)

    def project(k, slot):
        for p in range(parts):
            cols = slice(p * f + k * c, p * f + (k + 1) * c)
            t_refs[slot][:, p * c:(p + 1) * c] = jnp.dot(h_ref[...], win_ref[:, cols], preferred_element_type=F32)

    def mix(k, slot):
        t_ref = t_refs[slot]
        tile_rows = pl.ds(HALO, ts)
        chunk = slice(k * c, (k + 1) * c)
        if is_ffn:
            conv = _conv3(t_ref[:, 0:c], cw_ref[:, chunk], ts) + cb_ref[:, chunk]
            y = _silu(conv) * t_ref[tile_rows, c:2 * c]
        else:
            conv = _conv3(t_ref[:, c:2 * c] * t_ref[:, 2 * c:3 * c], cw_ref[:, chunk], ts) + cb_ref[:, chunk]
            y = t_ref[tile_rows, 0:c] * conv
        acc_ref[...] += jnp.dot(y.astype(BF16), wout_ref[chunk, :], preferred_element_type=F32)

    _software_pipeline(f // c, project, mix, unroll=True)
    o_ref[0] = xm_ref[0] + g_ref[0] * acc_ref[...]


def _gated_conv_call(x, sh, sc, g, ng, win, cw, cb, wout, *, is_ffn, name):
    b, s, d = x.shape
    ts = _tile(s, 1024)
    f = wout.shape[0]
    c = MXU_WIDTH if f % MXU_WIDTH == 0 else LANES
    full = lambda a: pl.BlockSpec(a.shape, lambda bi, i: (0,) * a.ndim)
    once = lambda a: pl.BlockSpec(a.shape, lambda bi, i: (0,) * a.ndim, pipeline_mode=pl.Buffered(1))
    vec = pl.BlockSpec((1, 1, d), lambda bi, i: (bi, 0, 0))
    t_buf = pltpu.VMEM((ts + 2 * HALO, win.shape[1] // f * c), F32)
    return pl.pallas_call(
        functools.partial(_gated_conv_kernel, is_ffn=is_ffn, c=c), grid=(b, s // ts),
        in_specs=_halo_specs(ts, s, d) + [vec, vec, vec, full(ng), once(win), full(cw), full(cb), once(wout)],
        out_specs=pl.BlockSpec((1, ts, d), lambda bi, i: (bi, i, 0)),
        out_shape=jax.ShapeDtypeStruct((b, s, d), F32),
        scratch_shapes=[pltpu.VMEM((ts, d), F32), pltpu.VMEM((ts + 2 * HALO, d), BF16), t_buf, t_buf],
        compiler_params=_params(("parallel", "arbitrary")), name=name,
    )(x, x, x, sh, sc, g, ng, win, cw, cb, wout)


def _rope_partner():
    quarter = QK_ROPE // 4
    partner = list(range(LANES))
    for g in range(QK_NOPE, QK_DIM, 2 * quarter):
        for l in range(g, g + quarter):
            partner[l], partner[l + quarter] = l + quarter, l
    return jnp.array(partner, jnp.int32)


def _rope_tables(n_tokens):
    half = QK_ROPE // 2
    quarter = half // 2
    inv = ROPE_THETA ** (-jnp.arange(0, half, 2, dtype=F32) / half)
    lane = jnp.arange(LANES)
    on = jnp.logical_and(lane >= QK_NOPE, lane < QK_DIM)[None, :]
    r = jnp.clip(lane - QK_NOPE, 0, QK_ROPE - 1)
    use_col = (r >= half)[None, :]
    second = ((r % half) >= quarter)[None, :]
    pos = jnp.arange(n_tokens)
    p = jnp.where(use_col, (pos % GRID_W)[:, None], (pos // GRID_W)[:, None]).astype(F32)
    ang = p * inv[r % quarter][None, :]
    sin = jnp.sin(ang)
    return jnp.where(on, jnp.cos(ang), 1.0), jnp.where(on, jnp.where(second, sin, -sin), 0.0)


def _pad_heads(w, heads, width):
    k = w.shape[0]
    w = w.reshape(k, heads, width)
    return jnp.pad(w, ((0, 0), (0, 0), (0, LANES - width))).reshape(k, heads * LANES)


def kernel(x, c, ctx, c_ctx, ada_w, ada_b, norm_mix_g, norm_ffn_g, ffn_w_up, ffn_conv_w, ffn_conv_b, ffn_w_down,
           ev_w_in, ev_conv_w, ev_conv_b, ev_ln_g, ev_ln_b, ev_qa_norm_g, ev_w_uq, ev_kva_norm_g, ev_w_ukv,
           ev_q_norm_g, ev_k_norm_g, ev_w_out, od_w_in, od_conv_w, od_conv_b, od_w_out):
    b, s, d = x.shape
    n_ctx = ctx.shape[1]
    depth = ada_w.shape[0]
    assert depth == 2, "kernel implements the two-layer (even, odd) block"
    conv_ch = ev_conv_w.shape[2]
    q_lora = ev_qa_norm_g.shape[1]
    kv_lora = ev_kva_norm_g.shape[1]
    heads = ev_w_uq.shape[2] // QK_DIM
    assert s % GRID_W == 0 and s % LANES == 0 and n_ctx % LANES == 0 and heads % 2 == 0
    assert ev_w_in.shape[2] == 2 * conv_ch + q_lora + kv_lora + QK_ROPE

    rows = -(-(b + 1) // HALO) * HALO
    cvec = jnp.concatenate([c, c_ctx[None, :], jnp.zeros((rows - b - 1, d), F32)], axis=0)
    mod = _mod_call(cvec, ada_w, ada_b)
    lat = lambda l, k: mod[l, :b, None, k * d:(k + 1) * d]
    ctxm = lambda l, k: jnp.broadcast_to(mod[l, b:b + 1, None, k * d:(k + 1) * d], (b, 1, d))

    w_in = ev_w_in[0]
    o_kr = 2 * conv_ch + q_lora + kv_lora
    partner = _rope_partner()
    kr_tile = jnp.pad(w_in[:, o_kr:], ((0, 0), (QK_NOPE, LANES - QK_DIM)))
    w1 = jnp.concatenate([w_in[:, :o_kr], kr_tile, kr_tile[:, partner]], axis=1).astype(BF16)
    wq = _pad_heads(ev_w_uq[0], heads, QK_DIM)
    wq_partner = wq.reshape(q_lora, heads, LANES)[:, :, partner].reshape(q_lora, heads * LANES)
    wq = jnp.concatenate([wq, wq_partner], axis=1).astype(BF16)
    wkv = ev_w_ukv[0].reshape(kv_lora, heads, QK_NOPE + V_DIM)
    wk = jnp.pad(wkv[:, :, :QK_NOPE], ((0, 0), (0, 0), (0, LANES - QK_NOPE))).reshape(kv_lora, heads * LANES)
    wk = wk.astype(BF16)
    wv3 = wkv[:, :, QK_NOPE:]
    zv = jnp.zeros_like(wv3)
    even_head = (jnp.arange(heads) % 2 == 0)[None, :, None]
    wv = jnp.where(even_head, jnp.concatenate([wv3, zv], -1), jnp.concatenate([zv, wv3], -1))
    wv = wv.reshape(kv_lora, heads * LANES).astype(BF16)
    gq = jnp.pad(ev_q_norm_g[0], (0, LANES - QK_DIM))[None, :]
    gk = jnp.pad(ev_k_norm_g[0], (0, LANES - QK_DIM))[None, :]
    even_kw = dict(heads=heads, conv_ch=conv_ch, q_lora=q_lora, kv_lora=kv_lora)
    score_bound = QK_DIM * jnp.max(jnp.abs(ev_q_norm_g[0])) * jnp.max(jnp.abs(ev_k_norm_g[0])) * Q_SCALE
    static_shift = score_bound < MAX_STATIC_SHIFT
    kbias = jnp.where(jnp.logical_and(static_shift, jnp.arange(LANES) == BIAS_LANE), -score_bound, 0.0)[None, :]
    even_w = (norm_mix_g[0:1], w1, ev_qa_norm_g, wq, ev_kva_norm_g, wk, wv, gq, gq[:, partner], gk, gk[:, partner],
              kbias)

    cos, ssin = _rope_tables(s)
    ident = (jnp.ones((n_ctx, LANES), F32), jnp.zeros((n_ctx, LANES), F32))

    ktc, vc = _even_in_call(ctx, ctxm(0, 0), ctxm(0, 1), *even_w, *ident, with_q=False, name="even_in_ctx",
                            **even_kw)
    u, q, kt, v = _even_in_call(x, lat(0, 0), lat(0, 1), *even_w, cos, ssin, with_q=True, name="even_in",
                                **even_kw)
    att, a = lax.cond(static_shift, functools.partial(_attn_call, online_max=False),
                      functools.partial(_attn_call, online_max=True),
                      q, ktc, vc, kt, v, u, ev_conv_w[0], ev_conv_b, ev_ln_g, ev_ln_b)
    w_out = ev_w_out[0].astype(BF16)
    x = _out_proj_call(x, a, att, lat(0, 2), w_out[:conv_ch], w_out[conv_ch:])

    def ffn(xin, l):
        return _gated_conv_call(xin, lat(l, 3), lat(l, 4), lat(l, 5), norm_ffn_g[l:l + 1], ffn_w_up[l].astype(BF16),
                                ffn_conv_w[l], ffn_conv_b[l:l + 1], ffn_w_down[l].astype(BF16),
                                is_ffn=True, name="ffn")

    x = ffn(x, 0)

    x = _gated_conv_call(x, lat(1, 0), lat(1, 1), lat(1, 2), norm_mix_g[1:2], od_w_in[0].astype(BF16),
                         od_conv_w[0], od_conv_b, od_w_out[0].astype(BF16), is_ffn=False, name="odd_mix")
    return ffn(x, 1)
```

```python
import functools
import math

import jax
import jax.numpy as jnp
from jax import lax
from jax.experimental import pallas as pl
from jax.experimental.pallas import tpu as pltpu

F32 = jnp.float32
BF16 = jnp.bfloat16

GRID_W = 64
QK_NOPE = 64
QK_ROPE = 32
V_DIM = 64
QK_DIM = QK_NOPE + QK_ROPE
BIAS_LANE = QK_DIM
ROPE_THETA = 10000.0
EPS = 1e-6
SM_SCALE = QK_DIM ** -0.5
Q_SCALE = SM_SCALE * math.log2(math.e)
N_MOD = 6

LANES = 128
SUBLANES = 8
MXU_WIDTH = 256
VMEM_LIMIT = 60000 * 1024

ATTN_KB = 1024
ATTN_GROUP = 4
CONF_ROWS = 32
HALO = SUBLANES
CONF_HALO = 16
NEG_BIG = -1e30
MAX_STATIC_SHIFT = 40.0


def _tile(n, pref):
    t = min(n, pref)
    while n % t:
        t //= 2
    return t


def _rms(x, g, n):
    ms = jnp.sum(x * x, axis=-1, keepdims=True) * (1.0 / n)
    return x * lax.rsqrt(ms + EPS) * g


def _silu(x):
    return x * jax.nn.sigmoid(x)


def _params(sem):
    return pltpu.CompilerParams(dimension_semantics=sem, vmem_limit_bytes=VMEM_LIMIT)


def _software_pipeline(n, produce, consume, *, group=2, unroll=False, filler=None):
    assert group % 2 == 0
    produce(0, 0)
    n_loop = 0 if unroll else (n - 1) // group

    def body(u, carry):
        for j in range(group):
            produce(group * u + j + 1, (j + 1) % 2)
            consume(group * u + j, j % 2)
        if filler is not None:
            filler(u)
        return carry
    if n_loop:
        lax.fori_loop(0, n_loop, body, 0)
    for t in range(group * n_loop, n):
        if t + 1 < n:
            produce(t + 1, (t + 1) % 2)
        consume(t, t % 2)
    if filler is not None:
        filler(n_loop)


def _pipeline_groups(n, group):
    return (n - 1) // group + 1


def _mod_kernel(c_ref, w_ref, b_ref, o_ref):
    s = _silu(c_ref[...])
    o_ref[0] = jnp.dot(s, w_ref[0], preferred_element_type=F32,
                       precision=lax.Precision.HIGHEST) + b_ref[0]


def _mod_call(cvec, ada_w, ada_b):
    depth, d, nd = ada_w.shape
    rows = cvec.shape[0]
    tn = _tile(nd, 1536)
    return pl.pallas_call(
        _mod_kernel,
        grid=(depth, nd // tn),
        in_specs=[
            pl.BlockSpec((rows, d), lambda l, j: (0, 0)),
            pl.BlockSpec((1, d, tn), lambda l, j: (l, 0, j)),
            pl.BlockSpec((1, 1, tn), lambda l, j: (l, 0, j)),
        ],
        out_specs=pl.BlockSpec((1, rows, tn), lambda l, j: (l, 0, j)),
        out_shape=jax.ShapeDtypeStruct((depth, rows, nd), F32),
        compiler_params=_params(("arbitrary", "arbitrary")),
        name="mod",
    )(cvec, ada_w, ada_b.reshape(depth, 1, nd))


def _even_in_kernel(x_ref, sh_ref, sc_ref, gmix_ref, w1_ref, qag_ref, wq_ref, kvag_ref, wk_ref, wv_ref,
                    gq_ref, gqp_ref, gk_ref, gkp_ref, kbias_ref, cos_ref, ssin_ref, *out_refs, heads, conv_ch, q_lora,
                    kv_lora, with_q):
    if with_q:
        u_ref, q_ref, kt_ref, v_ref = out_refs
    else:
        kt_ref, v_ref = out_refs
    ts, d = x_ref.shape[1], x_ref.shape[2]
    x = x_ref[0]
    h = _rms(x, gmix_ref[...], d) * (1.0 + sc_ref[0]) + sh_ref[0]
    proj = jnp.dot(h.astype(BF16), w1_ref[...], preferred_element_type=F32)
    o_cq = 2 * conv_ch
    o_ckv = o_cq + q_lora
    o_kr = o_ckv + kv_lora
    cos, ssin = cos_ref[...], ssin_ref[...]
    lane = lax.broadcasted_iota(jnp.int32, (1, LANES), 1)

    if with_q:
        u_ref[0] = proj[:, :conv_ch] * jax.nn.sigmoid(proj[:, conv_ch:o_cq])
        cqn = _rms(proj[:, o_cq:o_ckv], qag_ref[...], q_lora)
        qf = jnp.dot(cqn.astype(BF16), wq_ref[...], preferred_element_type=F32)
        q_cos = gq_ref[...] * cos
        q_sin = gqp_ref[...] * ssin
        bias_one = jnp.where(lane == BIAS_LANE, 1.0, 0.0)
        for hd in range(heads):
            qh = qf[:, hd * LANES:(hd + 1) * LANES]
            qp = qf[:, (heads + hd) * LANES:(heads + hd + 1) * LANES]
            r = lax.rsqrt(jnp.sum(qh * qh, axis=-1, keepdims=True) * (1.0 / QK_DIM) + EPS)
            qr = (qh * q_cos + qp * q_sin) * (r * Q_SCALE)
            q_ref[0, hd] = (qr + bias_one).astype(BF16)

    ckvn = _rms(proj[:, o_ckv:o_kr], kvag_ref[...], kv_lora).astype(BF16)
    kf = jnp.dot(ckvn, wk_ref[...], preferred_element_type=F32)
    vf = jnp.dot(ckvn, wv_ref[...], preferred_element_type=F32)
    kr = proj[:, o_kr:o_kr + LANES]
    krp = proj[:, o_kr + LANES:o_kr + 2 * LANES]
    gk = gk_ref[...]
    k_rot = kr * (gk * cos) + krp * (gkp_ref[...] * ssin)
    ss_rot = jnp.sum(kr * kr, axis=-1, keepdims=True)
    for hd in range(heads):
        kn = kf[:, hd * LANES:(hd + 1) * LANES]
        r = lax.rsqrt((jnp.sum(kn * kn, axis=-1, keepdims=True) + ss_rot) * (1.0 / QK_DIM) + EPS)
        kt_ref[0, hd, 0] = ((kn * gk + k_rot) * r + kbias_ref[...]).T.astype(BF16)
        one_hot = jnp.where(lane == _sum_lane(hd), 1.0, 0.0)
        v_ref[0, hd] = (vf[:, hd * LANES:(hd + 1) * LANES] + one_hot).astype(BF16)


def _sum_lane(head):
    return V_DIM if head % 2 == 0 else 0


def _even_in_call(x, sh, sc, gmix, w1, qag, wq, kvag, wk, wv, gq, gqp, gk, gkp, kbias, cos, ssin, *, heads, conv_ch,
                  q_lora, kv_lora, with_q, name):
    b, s, d = x.shape
    ts = _tile(s, 512)
    kb = _tile(s, ATTN_KB)
    r = kb // ts
    kern = functools.partial(_even_in_kernel, heads=heads, conv_ch=conv_ch, q_lora=q_lora, kv_lora=kv_lora,
                             with_q=with_q)
    full = lambda a: pl.BlockSpec(a.shape, lambda bi, i: (0,) * a.ndim)
    vec = pl.BlockSpec((1, 1, d), lambda bi, i: (bi, 0, 0))
    tab = pl.BlockSpec((ts, LANES), lambda bi, i: (i, 0))
    in_specs = [pl.BlockSpec((1, ts, d), lambda bi, i: (bi, i, 0)), vec, vec, full(gmix), full(w1), full(qag),
                full(wq), full(kvag), full(wk), full(wv), full(gq), full(gqp), full(gk), full(gkp), full(kbias),
                tab, tab]
    head_rows = pl.BlockSpec((1, heads, ts, LANES), lambda bi, i: (bi, 0, i, 0))
    out_specs = [pl.BlockSpec((1, heads, 1, LANES, ts), lambda bi, i: (bi, 0, i // r, 0, i % r)), head_rows]
    out_shape = [jax.ShapeDtypeStruct((b, heads, s // kb, LANES, kb), BF16),
                 jax.ShapeDtypeStruct((b, heads, s, LANES), BF16)]
    if with_q:
        out_specs = [pl.BlockSpec((1, ts, conv_ch), lambda bi, i: (bi, i, 0)), head_rows] + out_specs
        out_shape = [jax.ShapeDtypeStruct((b, s, conv_ch), F32),
                     jax.ShapeDtypeStruct((b, heads, s, LANES), BF16)] + out_shape
    return pl.pallas_call(
        kern, grid=(b, s // ts), in_specs=in_specs, out_specs=out_specs, out_shape=out_shape,
        compiler_params=_params(("parallel", "arbitrary")), name=name,
    )(x, sh, sc, gmix, w1, qag, wq, kvag, wk, wv, gq, gqp, gk, gkp, kbias, cos, ssin)


def _softmax_pv(h, s, v_blk, acc_ref, m_ref, online_max):
    if not online_max:
        acc_ref[h] += jnp.dot(s, v_blk, preferred_element_type=F32)
        return
    nt = s.shape[1] // LANES
    tile = lambda t: s[:, t * LANES:(t + 1) * LANES]
    mx = tile(0)
    for t in range(1, nt):
        mx = jnp.maximum(mx, tile(t))
    m_old = m_ref[h]
    m_new = jnp.maximum(m_old, jnp.max(mx, axis=-1, keepdims=True))
    m_ref[h] = m_new
    pm = jnp.concatenate([jnp.exp2(tile(t) - m_new).astype(BF16) for t in range(nt)], axis=1)
    pv = jnp.dot(pm, v_blk, preferred_element_type=F32)
    acc_ref[h] = acc_ref[h] * jnp.exp2(m_old - m_new) + pv


def _conf_conv_rows(r0, buf_ref, win_ref, w_ref, b_ref, g_ref, beta_ref, a_ref):
    taps = w_ref.shape[0]
    first = CONF_HALO - (taps - 1) // 2
    r0 = pl.multiple_of(r0, CONF_ROWS)
    span = CONF_ROWS + (first + taps - 1) // SUBLANES * SUBLANES
    window = buf_ref[pl.ds(r0, span + SUBLANES), :]
    for j in range(1, SUBLANES):
        win_ref[j - 1] = window[j:j + span]
    acc = jnp.zeros((CONF_ROWS, buf_ref.shape[1]), F32)
    for k in range(taps):
        q, j = divmod(first + k, SUBLANES)
        if j == 0:
            tap = buf_ref[pl.ds(r0 + q * SUBLANES, CONF_ROWS), :]
        else:
            tap = win_ref[j - 1, q * SUBLANES:q * SUBLANES + CONF_ROWS, :]
        acc = acc + w_ref[k:k + 1, :] * tap
    u = acc + b_ref[...]
    mu = jnp.mean(u, axis=-1, keepdims=True)
    xc = u - mu
    y = xc * lax.rsqrt(jnp.mean(xc * xc, axis=-1, keepdims=True) + EPS) * g_ref[...] + beta_ref[...]
    a_ref[0, pl.ds(r0, CONF_ROWS), :] = _silu(y).astype(a_ref.dtype)


def _attn_kernel(q_ref, ktc_ref, vc_ref, kt_ref, v_ref, um_ref, up_ref, un_ref, cw_ref, cb_ref, lg_ref, lb_ref,
                 o_ref, a_ref, acc_ref, m_ref, s0_ref, s1_ref, c0_ref, c1_ref, buf_ref, win_ref, *, online_max):
    heads, nkb, kb = kt_ref.shape[1], kt_ref.shape[2], kt_ref.shape[4]
    tq = q_ref.shape[2]
    n_items = heads * nkb
    s_refs = (s0_ref, s1_ref)
    c_refs = (c0_ref, c1_ref)
    i, n = pl.program_id(1), pl.num_programs(1)
    acc_ref[...] = jnp.zeros(acc_ref.shape, F32)
    post = (lambda sc: sc) if online_max else (lambda sc: jnp.exp2(sc).astype(BF16))
    if online_max:
        m_ref[...] = jnp.full(m_ref.shape, NEG_BIG, F32)
    buf_ref[0:CONF_HALO] = jnp.where(i > 0, up_ref[0], 0.0)
    buf_ref[CONF_HALO:CONF_HALO + tq] = um_ref[0]
    buf_ref[CONF_HALO + tq:] = jnp.where(i < n - 1, un_ref[0], 0.0)

    def ctx_scores(h, slot):
        c_refs[slot][...] = post(jnp.dot(q_ref[0, h], ktc_ref[0, h, 0], preferred_element_type=F32))

    def ctx_fold(h, slot):
        _softmax_pv(h, c_refs[slot][...], vc_ref[0, h], acc_ref, m_ref, online_max)

    _software_pipeline(heads, ctx_scores, ctx_fold)

    def scores(t, slot):
        s_refs[slot][...] = post(jnp.dot(q_ref[0, t // nkb], kt_ref[0, t // nkb, t % nkb],
                                         preferred_element_type=F32))

    def fold(t, slot):
        rows = pl.ds(pl.multiple_of((t % nkb) * kb, kb), kb)
        _softmax_pv(t // nkb, s_refs[slot][...], v_ref[0, t // nkb, rows, :], acc_ref, m_ref, online_max)

    rows_per_body = tq // _pipeline_groups(n_items, ATTN_GROUP)

    def conf_rows(g):
        for c in range(rows_per_body // CONF_ROWS):
            _conf_conv_rows(g * rows_per_body + c * CONF_ROWS, buf_ref, win_ref, cw_ref, cb_ref, lg_ref, lb_ref,
                            a_ref)

    _software_pipeline(n_items, scores, fold, group=ATTN_GROUP, filler=conf_rows)

    lane = lax.broadcasted_iota(jnp.int32, (1, LANES), 1)
    for p in range(heads // 2):
        even, odd = acc_ref[2 * p], acc_ref[2 * p + 1]
        l_even = even[:, _sum_lane(0):_sum_lane(0) + 1]
        l_odd = odd[:, _sum_lane(1):_sum_lane(1) + 1]
        o_ref[0, :, p * LANES:(p + 1) * LANES] = jnp.where(lane < V_DIM, even / l_even, odd / l_odd).astype(
            o_ref.dtype)


def _attn_call(q, ktc, vc, kt, v, u, cw, cb, lg, lb, *, online_max):
    b, heads, s, _ = q.shape
    ch = u.shape[2]
    tq = _tile(s, 512)
    kb = kt.shape[4]
    taps = cw.shape[0]
    groups = _pipeline_groups(heads * kt.shape[2], ATTN_GROUP)
    assert (taps - 1) // 2 <= CONF_HALO and tq % (groups * CONF_ROWS) == 0
    hb, nh = tq // CONF_HALO, s // CONF_HALO
    span = CONF_ROWS + (CONF_HALO - (taps - 1) // 2 + taps - 1) // SUBLANES * SUBLANES
    s_dtype = F32 if online_max else BF16
    resident =lambda a: pl.BlockSpec((1,) + a.shape[1:], lambda bi, i: (bi,) + (0,) * (a.ndim - 1),
                                      pipeline_mode=pl.Buffered(1))
    full = lambda a: pl.BlockSpec(a.shape, lambda bi, i: (0,) * a.ndim)
    rows = lambda c: pl.BlockSpec((1, tq, c), lambda bi, i: (bi, i, 0))
    return pl.pallas_call(
        functools.partial(_attn_kernel, online_max=online_max), grid=(b, s // tq),
        in_specs=[pl.BlockSpec((1, heads, tq, LANES), lambda bi, i: (bi, 0, i, 0)),
                  resident(ktc), resident(vc), resident(kt), resident(v),
                  rows(ch),
                  pl.BlockSpec((1, CONF_HALO, ch), lambda bi, i: (bi, jnp.maximum(i * hb - 1, 0), 0)),
                  pl.BlockSpec((1, CONF_HALO, ch), lambda bi, i: (bi, jnp.minimum((i + 1) * hb, nh - 1), 0)),
                  full(cw), full(cb), full(lg), full(lb)],
        out_specs=[rows(heads * V_DIM), rows(ch)],
        out_shape=[jax.ShapeDtypeStruct((b, s, heads * V_DIM), BF16), jax.ShapeDtypeStruct((b, s, ch), BF16)],
        scratch_shapes=[pltpu.VMEM((heads, tq, LANES), F32), pltpu.VMEM((heads, tq, LANES), F32),
                        pltpu.VMEM((tq, kb), s_dtype), pltpu.VMEM((tq, kb), s_dtype),
                        pltpu.VMEM((tq, ktc.shape[4]), s_dtype), pltpu.VMEM((tq, ktc.shape[4]), s_dtype),
                        pltpu.VMEM((tq + 2 * CONF_HALO, ch), F32), pltpu.VMEM((SUBLANES - 1, span, ch), F32)],
        compiler_params=_params(("parallel", "arbitrary")), name="attn_online" if online_max else "attn",
    )(q, ktc, vc, kt, v, u, u, u, cw, cb, lg, lb)


def _out_proj_kernel(x_ref, a_ref, t_ref, g_ref, wa_ref, wt_ref, o_ref):
    y = jnp.dot(a_ref[0], wa_ref[...], preferred_element_type=F32)
    y = y + jnp.dot(t_ref[0], wt_ref[...], preferred_element_type=F32)
    o_ref[0] = x_ref[0] + g_ref[0] * y


def _out_proj_call(x, a, att, g, wa, wt):
    b, s, d = x.shape
    ts = _tile(s, 1024)
    row = lambda c: pl.BlockSpec((1, ts, c), lambda bi, i: (bi, i, 0))
    full = lambda arr: pl.BlockSpec(arr.shape, lambda bi, i: (0,) * arr.ndim)
    return pl.pallas_call(
        _out_proj_kernel, grid=(b, s // ts),
        in_specs=[row(d), row(a.shape[2]), row(att.shape[2]),
                  pl.BlockSpec((1, 1, d), lambda bi, i: (bi, 0, 0)), full(wa), full(wt)],
        out_specs=row(d), out_shape=jax.ShapeDtypeStruct((b, s, d), F32),
        compiler_params=_params(("parallel", "arbitrary")), name="out_proj",
    )(x, a, att, g, wa, wt)


def _halo_hidden(xm_ref, xp_ref, xn_ref, sh_ref, sc_ref, ng_ref):
    i, n = pl.program_id(1), pl.num_programs(1)
    ts, d = xm_ref.shape[1], xm_ref.shape[2]
    xe = jnp.concatenate([xp_ref[0], xm_ref[0], xn_ref[0]], axis=0)
    h = _rms(xe, ng_ref[...], d) * (1.0 + sc_ref[0]) + sh_ref[0]
    row = lax.broadcasted_iota(jnp.int32, (ts + 2 * HALO, 1), 0)
    lo = jnp.where(i > 0, 0, HALO)
    hi = jnp.where(i < n - 1, ts + 2 * HALO, ts + HALO)
    valid = jnp.logical_and(row >= lo, row < hi)
    return jnp.where(valid, h, 0.0).astype(BF16)


def _conv3(z, w, ts):
    rows = z.shape[0]
    y = w[0:1] * pltpu.roll(z, 1, 0) + w[1:2] * z + w[2:3] * pltpu.roll(z, rows - 1, 0)
    return y[HALO:HALO + ts]


def _halo_specs(ts, s, d):
    hb, nh = ts // HALO, s // HALO
    return [pl.BlockSpec((1, ts, d), lambda bi, i: (bi, i, 0)),
            pl.BlockSpec((1, HALO, d), lambda bi, i: (bi, jnp.maximum(i * hb - 1, 0), 0)),
            pl.BlockSpec((1, HALO, d), lambda bi, i: (bi, jnp.minimum((i + 1) * hb, nh - 1), 0))]


def _gated_conv_kernel(xm_ref, xp_ref, xn_ref, sh_ref, sc_ref, g_ref, ng_ref, win_ref, cw_ref, cb_ref, wout_ref,
                       o_ref, acc_ref, h_ref, t0_ref, t1_ref, *, is_ffn, c):
    ts = xm_ref.shape[1]
    f = wout_ref.shape[0]
    parts = win_ref.shape[1] // f
    t_refs = (t0_ref, t1_ref)
    h_ref[...] = _halo_hidden(xm_ref, xp_ref, xn_ref, sh_ref, sc_ref, ng_ref)
    acc_ref[...] = jnp.zeros(acc_ref.shape, F32)

    def project(k, slot):
        for p in range(parts):
            cols = slice(p * f + k * c, p * f + (k + 1) * c)
            t_refs[slot][:, p * c:(p + 1) * c] = jnp.dot(h_ref[...], win_ref[:, cols], preferred_element_type=F32)

    def mix(k, slot):
        t_ref = t_refs[slot]
        tile_rows = pl.ds(HALO, ts)
        chunk = slice(k * c, (k + 1) * c)
        if is_ffn:
            conv = _conv3(t_ref[:, 0:c], cw_ref[:, chunk], ts) + cb_ref[:, chunk]
            y = _silu(conv) * t_ref[tile_rows, c:2 * c]
        else:
            conv = _conv3(t_ref[:, c:2 * c] * t_ref[:, 2 * c:3 * c], cw_ref[:, chunk], ts) + cb_ref[:, chunk]
            y = t_ref[tile_rows, 0:c] * conv
        acc_ref[...] += jnp.dot(y.astype(BF16), wout_ref[chunk, :], preferred_element_type=F32)

    _software_pipeline(f // c, project, mix, unroll=True)
    o_ref[0] = xm_ref[0] + g_ref[0] * acc_ref[...]


def _gated_conv_call(x, sh, sc, g, ng, win, cw, cb, wout, *, is_ffn, name):
    b, s, d = x.shape
    ts = _tile(s, 1024)
    f = wout.shape[0]
    c = MXU_WIDTH if f % MXU_WIDTH == 0 else LANES
    full = lambda a: pl.BlockSpec(a.shape, lambda bi, i: (0,) * a.ndim)
    once = lambda a: pl.BlockSpec(a.shape, lambda bi, i: (0,) * a.ndim, pipeline_mode=pl.Buffered(1))
    vec = pl.BlockSpec((1, 1, d), lambda bi, i: (bi, 0, 0))
    t_buf = pltpu.VMEM((ts + 2 * HALO, win.shape[1] // f * c), F32)
    return pl.pallas_call(
        functools.partial(_gated_conv_kernel, is_ffn=is_ffn, c=c), grid=(b, s // ts),
        in_specs=_halo_specs(ts, s, d) + [vec, vec, vec, full(ng), once(win), full(cw), full(cb), once(wout)],
        out_specs=pl.BlockSpec((1, ts, d), lambda bi, i: (bi, i, 0)),
        out_shape=jax.ShapeDtypeStruct((b, s, d), F32),
        scratch_shapes=[pltpu.VMEM((ts, d), F32), pltpu.VMEM((ts + 2 * HALO, d), BF16), t_buf, t_buf],
        compiler_params=_params(("parallel", "arbitrary")), name=name,
    )(x, x, x, sh, sc, g, ng, win, cw, cb, wout)


def _rope_partner():
    quarter = QK_ROPE // 4
    partner = list(range(LANES))
    for g in range(QK_NOPE, QK_DIM, 2 * quarter):
        for l in range(g, g + quarter):
            partner[l], partner[l + quarter] = l + quarter, l
    return jnp.array(partner, jnp.int32)


def _rope_tables(n_tokens):
    half = QK_ROPE // 2
    quarter = half // 2
    inv = ROPE_THETA ** (-jnp.arange(0, half, 2, dtype=F32) / half)
    lane = jnp.arange(LANES)
    on = jnp.logical_and(lane >= QK_NOPE, lane < QK_DIM)[None, :]
    r = jnp.clip(lane - QK_NOPE, 0, QK_ROPE - 1)
    use_col = (r >= half)[None, :]
    second = ((r % half) >= quarter)[None, :]
    pos = jnp.arange(n_tokens)
    p = jnp.where(use_col, (pos % GRID_W)[:, None], (pos // GRID_W)[:, None]).astype(F32)
    ang = p * inv[r % quarter][None, :]
    sin = jnp.sin(ang)
    return jnp.where(on, jnp.cos(ang), 1.0), jnp.where(on, jnp.where(second, sin, -sin), 0.0)


def _pad_heads(w, heads, width):
    k = w.shape[0]
    w = w.reshape(k, heads, width)
    return jnp.pad(w, ((0, 0), (0, 0), (0, LANES - width))).reshape(k, heads * LANES)


def kernel(x, c, ctx, c_ctx, ada_w, ada_b, norm_mix_g, norm_ffn_g, ffn_w_up, ffn_conv_w, ffn_conv_b, ffn_w_down,
           ev_w_in, ev_conv_w, ev_conv_b, ev_ln_g, ev_ln_b, ev_qa_norm_g, ev_w_uq, ev_kva_norm_g, ev_w_ukv,
           ev_q_norm_g, ev_k_norm_g, ev_w_out, od_w_in, od_conv_w, od_conv_b, od_w_out):
    b, s, d = x.shape
    n_ctx = ctx.shape[1]
    depth = ada_w.shape[0]
    assert depth == 2, "kernel implements the two-layer (even, odd) block"
    conv_ch = ev_conv_w.shape[2]
    q_lora = ev_qa_norm_g.shape[1]
    kv_lora = ev_kva_norm_g.shape[1]
    heads = ev_w_uq.shape[2] // QK_DIM
    assert s % GRID_W == 0 and s % LANES == 0 and n_ctx % LANES == 0 and heads % 2 == 0
    assert ev_w_in.shape[2] == 2 * conv_ch + q_lora + kv_lora + QK_ROPE

    rows = -(-(b + 1) // HALO) * HALO
    cvec = jnp.concatenate([c, c_ctx[None, :], jnp.zeros((rows - b - 1, d), F32)], axis=0)
    mod = _mod_call(cvec, ada_w, ada_b)
    lat = lambda l, k: mod[l, :b, None, k * d:(k + 1) * d]
    ctxm = lambda l, k: jnp.broadcast_to(mod[l, b:b + 1, None, k * d:(k + 1) * d], (b, 1, d))

    w_in = ev_w_in[0]
    o_kr = 2 * conv_ch + q_lora + kv_lora
    partner = _rope_partner()
    kr_tile = jnp.pad(w_in[:, o_kr:], ((0, 0), (QK_NOPE, LANES - QK_DIM)))
    w1 = jnp.concatenate([w_in[:, :o_kr], kr_tile, kr_tile[:, partner]], axis=1).astype(BF16)
    wq = _pad_heads(ev_w_uq[0], heads, QK_DIM)
    wq_partner = wq.reshape(q_lora, heads, LANES)[:, :, partner].reshape(q_lora, heads * LANES)
    wq = jnp.concatenate([wq, wq_partner], axis=1).astype(BF16)
    wkv = ev_w_ukv[0].reshape(kv_lora, heads, QK_NOPE + V_DIM)
    wk = jnp.pad(wkv[:, :, :QK_NOPE], ((0, 0), (0, 0), (0, LANES - QK_NOPE))).reshape(kv_lora, heads * LANES)
    wk = wk.astype(BF16)
    wv3 = wkv[:, :, QK_NOPE:]
    zv = jnp.zeros_like(wv3)
    even_head = (jnp.arange(heads) % 2 == 0)[None, :, None]
    wv = jnp.where(even_head, jnp.concatenate([wv3, zv], -1), jnp.concatenate([zv, wv3], -1))
    wv = wv.reshape(kv_lora, heads * LANES).astype(BF16)
    gq = jnp.pad(ev_q_norm_g[0], (0, LANES - QK_DIM))[None, :]
    gk = jnp.pad(ev_k_norm_g[0], (0, LANES - QK_DIM))[None, :]
    even_kw = dict(heads=heads, conv_ch=conv_ch, q_lora=q_lora, kv_lora=kv_lora)
    score_bound = QK_DIM * jnp.max(jnp.abs(ev_q_norm_g[0])) * jnp.max(jnp.abs(ev_k_norm_g[0])) * Q_SCALE
    static_shift = score_bound < MAX_STATIC_SHIFT
    kbias = jnp.where(jnp.logical_and(static_shift, jnp.arange(LANES) == BIAS_LANE), -score_bound, 0.0)[None, :]
    even_w = (norm_mix_g[0:1], w1, ev_qa_norm_g, wq, ev_kva_norm_g, wk, wv, gq, gq[:, partner], gk, gk[:, partner],
              kbias)

    cos, ssin = _rope_tables(s)
    ident = (jnp.ones((n_ctx, LANES), F32), jnp.zeros((n_ctx, LANES), F32))

    ktc, vc = _even_in_call(ctx, ctxm(0, 0), ctxm(0, 1), *even_w, *ident, with_q=False, name="even_in_ctx",
                            **even_kw)
    u, q, kt, v = _even_in_call(x, lat(0, 0), lat(0, 1), *even_w, cos, ssin, with_q=True, name="even_in",
                                **even_kw)
    att, a = lax.cond(static_shift, functools.partial(_attn_call, online_max=False),
                      functools.partial(_attn_call, online_max=True),
                      q, ktc, vc, kt, v, u, ev_conv_w[0], ev_conv_b, ev_ln_g, ev_ln_b)
    w_out = ev_w_out[0].astype(BF16)
    x = _out_proj_call(x, a, att, lat(0, 2), w_out[:conv_ch], w_out[conv_ch:])

    def ffn(xin, l):
        return _gated_conv_call(xin, lat(l, 3), lat(l, 4), lat(l, 5), norm_ffn_g[l:l + 1], ffn_w_up[l].astype(BF16),
                                ffn_conv_w[l], ffn_conv_b[l:l + 1], ffn_w_down[l].astype(BF16),
                                is_ffn=True, name="ffn")

    x = ffn(x, 0)

    x = _gated_conv_call(x, lat(1, 0), lat(1, 1), lat(1, 2), norm_mix_g[1:2], od_w_in[0].astype(BF16),
                         od_conv_w[0], od_conv_b, od_w_out[0].astype(BF16), is_ffn=False, name="odd_mix")
    return ffn(x, 1)
```
